```python
import math
import jax, jax.numpy as jnp
from jax import lax
import numpy as np

D_MODEL = 2048
BATCH = 2
SEQ = 8192
DEPTH = 1

SSM_WIDTH = D_MODEL // 2
SSM_GROUP = 16
SSM_GROUPS = SSM_WIDTH // SSM_GROUP
SSM_STATE = 64
ATTN_HEAD_DIM = 128
ATTN_WIDTH = D_MODEL - SSM_WIDTH
ATTN_HEADS = ATTN_WIDTH // ATTN_HEAD_DIM
Q_BLOCK = 128
IN_COLS = SSM_WIDTH + 3 * ATTN_WIDTH + ATTN_HEADS
MEM_LEN = 256
CROSS_HEADS = 4
CROSS_HEAD_DIM = D_MODEL // CROSS_HEADS
N_EXPERTS = 32
TOP_K = 4
D_FF = D_MODEL
SWIGLU_ALPHA = 1.702
SWIGLU_LIMIT = 7.0
MOE_BLOCK = 256
RMS_EPS = 1e-5
DT_MIN = 1e-3
DT_MAX = 1e-1
FORGET_BIAS_INIT = 2.0

kernel_name = "hybrid_s5_fox_moe_layer"


def rmsnorm(x, g):
    xf = x.astype(jnp.float32)
    y = xf * lax.rsqrt(jnp.mean(xf * xf, axis=-1, keepdims=True) + RMS_EPS)
    return (y * g.astype(jnp.float32)).astype(x.dtype)


def _complex_affine_combine(e1, e2):
    a1r, a1i, b1r, b1i = e1
    a2r, a2i, b2r, b2i = e2
    ar = a2r * a1r - a2i * a1i
    ai = a2r * a1i + a2i * a1r
    br = a2r * b1r - a2i * b1i + b2r
    bi = a2r * b1i + a2i * b1r + b2i
    return (ar, ai, br, bi)


def s5_mixer(u, lam_re, lam_im, log_step, b_re, b_im, c_re, c_im, d_skip, w_glu, b_glu):
    bsz, length, _ = u.shape
    uf = u.astype(jnp.float32).reshape(bsz, length, SSM_GROUPS, SSM_GROUP)
    lr = lam_re.astype(jnp.float32)
    li = lam_im.astype(jnp.float32)
    dt = jnp.exp(log_step.astype(jnp.float32))[:, None]
    mag = jnp.exp(lr * dt)
    lb_re = mag * jnp.cos(li * dt)
    lb_im = mag * jnp.sin(li * dt)
    den = lr * lr + li * li
    nr = lb_re - 1.0
    ni = lb_im
    coef_re = (nr * lr + ni * li) / den
    coef_im = (ni * lr - nr * li) / den
    br = b_re.astype(jnp.float32)
    bi = b_im.astype(jnp.float32)
    bb_re = coef_re[..., None] * br - coef_im[..., None] * bi
    bb_im = coef_re[..., None] * bi + coef_im[..., None] * br
    bu_re = jnp.einsum('blgh,gph->blgp', uf, bb_re)
    bu_im = jnp.einsum('blgh,gph->blgp', uf, bb_im)
    a_re = jnp.broadcast_to(lb_re, bu_re.shape)
    a_im = jnp.broadcast_to(lb_im, bu_im.shape)
    _, _, xs_re, xs_im = lax.associative_scan(
        _complex_affine_combine, (a_re, a_im, bu_re, bu_im), axis=1)
    y = (jnp.einsum('blgp,ghp->blgh', xs_re, c_re.astype(jnp.float32))
         - jnp.einsum('blgp,ghp->blgh', xs_im, c_im.astype(jnp.float32))
         + d_skip.astype(jnp.float32).reshape(SSM_GROUPS, SSM_GROUP) * uf)
    y = jax.nn.gelu(y.reshape(bsz, length, SSM_WIDTH))
    y = y * jax.nn.sigmoid(y @ w_glu.astype(jnp.float32) + b_glu.astype(jnp.float32))
    return y.astype(u.dtype)


def fox_attention(q, k, v, f_logit):
    bsz, length, _ = q.shape
    def heads(t):
        return t.reshape(bsz, length, ATTN_HEADS, ATTN_HEAD_DIM).transpose(0, 2, 1, 3)
    q, k, v = heads(q), heads(k), heads(v)
    log_f = jax.nn.log_sigmoid(f_logit.astype(jnp.float32)).transpose(0, 2, 1)
    cum = jnp.cumsum(log_f, axis=-1)
    nb = length // Q_BLOCK
    q_blocks = q.reshape(bsz, ATTN_HEADS, nb, Q_BLOCK, ATTN_HEAD_DIM).transpose(2, 0, 1, 3, 4)
    c_blocks = cum.reshape(bsz, ATTN_HEADS, nb, Q_BLOCK).transpose(2, 0, 1, 3)
    kpos = jnp.arange(length)
    scale = ATTN_HEAD_DIM ** -0.5

    def one_block(args):
        qb, cb, i = args
        s = jnp.einsum('bhqd,bhkd->bhqk', qb, k, preferred_element_type=jnp.float32) * scale
        s = s + cb[..., None] - cum[:, :, None, :]
        qpos = i * Q_BLOCK + jnp.arange(Q_BLOCK)
        s = jnp.where(kpos[None, :] <= qpos[:, None], s, -jnp.inf)
        p = jax.nn.softmax(s, axis=-1)
        return jnp.einsum('bhqk,bhkd->bhqd', p.astype(v.dtype), v)

    o = lax.map(one_block, (q_blocks, c_blocks, jnp.arange(nb)))
    return o.transpose(1, 0, 3, 2, 4).reshape(bsz, length, ATTN_WIDTH)


def cross_attention(h, mem_n, w_cq, w_ckv, w_co):
    bsz, length, _ = h.shape
    q = (h @ w_cq).reshape(bsz, length, CROSS_HEADS, CROSS_HEAD_DIM)
    k, v = jnp.split(mem_n @ w_ckv, 2, axis=-1)
    k = k.reshape(bsz, -1, CROSS_HEADS, CROSS_HEAD_DIM)
    v = v.reshape(bsz, -1, CROSS_HEADS, CROSS_HEAD_DIM)
    s = jnp.einsum('blhd,bmhd->bhlm', q, k, preferred_element_type=jnp.float32) * CROSS_HEAD_DIM ** -0.5
    p = jax.nn.softmax(s, axis=-1)
    o = jnp.einsum('bhlm,bmhd->blhd', p.astype(v.dtype), v).reshape(bsz, length, D_MODEL)
    return o @ w_co


def moe_ffn(h, w_router, b_router, w_gu, b_gu, w_dn, b_dn):
    bsz, length, dm = h.shape
    n_tok = bsz * length
    n_asg = n_tok * TOP_K
    xf = h.reshape(n_tok, dm)
    logits = (xf @ w_router + b_router).astype(jnp.float32)
    top_v, top_i = lax.top_k(logits, TOP_K)
    gates = jax.nn.softmax(top_v, axis=-1)
    flat_e = top_i.reshape(-1)
    flat_g = gates.reshape(-1)
    flat_tok = jnp.arange(n_asg, dtype=jnp.int32) // TOP_K
    order = jnp.argsort(flat_e)
    sorted_e = flat_e[order]
    counts = jnp.bincount(flat_e, length=N_EXPERTS)
    padded = ((counts + MOE_BLOCK - 1) // MOE_BLOCK) * MOE_BLOCK
    pend = jnp.cumsum(padded)
    pstart = pend - padded
    cstart = jnp.cumsum(counts) - counts
    dest = pstart[sorted_e] + (jnp.arange(n_asg) - cstart[sorted_e])
    n_rows = ((n_asg + MOE_BLOCK - 1) // MOE_BLOCK) * MOE_BLOCK + N_EXPERTS * MOE_BLOCK
    n_blocks = n_rows // MOE_BLOCK
    row_tok = jnp.full((n_rows,), n_tok, jnp.int32).at[dest].set(flat_tok[order])
    row_w = jnp.zeros((n_rows,), jnp.float32).at[dest].set(flat_g[order])
    blk_e = jnp.clip(jnp.searchsorted(pend, jnp.arange(n_blocks) * MOE_BLOCK, side='right'),
                     0, N_EXPERTS - 1)
    xpad = jnp.concatenate([xf, jnp.zeros((1, dm), xf.dtype)], axis=0)
    xrows = xpad[row_tok].reshape(n_blocks, MOE_BLOCK, dm)

    def expert_block(args):
        xb, e = args
        gu = xb @ w_gu[e] + b_gu[e]
        g, lin = jnp.split(gu, 2, axis=-1)
        g = jnp.minimum(g, SWIGLU_LIMIT)
        lin = jnp.clip(lin, -SWIGLU_LIMIT, SWIGLU_LIMIT)
        act = g * jax.nn.sigmoid(SWIGLU_ALPHA * g) * (lin + 1.0)
        return act @ w_dn[e] + b_dn[e]

    yrows = lax.map(expert_block, (xrows, blk_e)).reshape(n_rows, dm)
    y = jax.ops.segment_sum(yrows * row_w[:, None].astype(yrows.dtype), row_tok,
                            num_segments=n_tok + 1)[:n_tok]
    return y.reshape(bsz, length, dm)


def setup_inputs(seed: int = 0) -> dict:
    key = jax.random.key(seed)
    ks = jax.random.split(key, 32)
    f32 = jnp.float32
    nrm = lambda k, shape, s: jax.random.normal(k, shape, f32) * s
    gain = lambda k, shape: 1.0 + 0.02 * jax.random.normal(k, shape, f32)
    L, D, G, P, H = DEPTH, D_MODEL, SSM_GROUPS, SSM_STATE, SSM_GROUP
    n_idx = jnp.arange(P, dtype=f32)
    return {
        'x': jax.random.normal(ks[0], (BATCH, SEQ, D), f32),
        'mem': jax.random.normal(ks[1], (BATCH, MEM_LEN, D), f32),
        'g_mix': gain(ks[2], (L, D)),
        'w_in': nrm(ks[3], (L, D, IN_COLS), D ** -0.5),
        'b_f': FORGET_BIAS_INIT + 0.5 * jax.random.normal(ks[4], (L, ATTN_HEADS), f32),
        'lam_re': -0.5 + 0.01 * jax.random.normal(ks[5], (L, G, P), f32),
        'lam_im': jnp.pi * n_idx + 0.01 * jax.random.normal(ks[6], (L, G, P), f32),
        'log_step': jax.random.uniform(ks[7], (L, G), f32, math.log(DT_MIN), math.log(DT_MAX)),
        'b_re': nrm(ks[8], (L, G, P, H), (2 * H) ** -0.5),
        'b_im': nrm(ks[9], (L, G, P, H), (2 * H) ** -0.5),
        'c_re': nrm(ks[10], (L, G, H, P), (2 * P) ** -0.5),
        'c_im': nrm(ks[11], (L, G, H, P), (2 * P) ** -0.5),
        'd_skip': nrm(ks[12], (L, SSM_WIDTH), 1.0),
        'w_glu': nrm(ks[13], (L, SSM_WIDTH, SSM_WIDTH), SSM_WIDTH ** -0.5),
        'b_glu': nrm(ks[14], (L, SSM_WIDTH), 0.01),
        'g_attn_out': gain(ks[15], (L, ATTN_WIDTH)),
        'g_ssm_out': gain(ks[16], (L, SSM_WIDTH)),
        'w_out': nrm(ks[17], (L, D, D), D ** -0.5),
        'g_cross': gain(ks[18], (L, D)),
        'g_mem': gain(ks[19], (L, D)),
        'w_cq': nrm(ks[20], (L, D, D), D ** -0.5),
        'w_ckv': nrm(ks[21], (L, D, 2 * D), D ** -0.5),
        'w_co': nrm(ks[22], (L, D, D), D ** -0.5),
        'g_moe': gain(ks[23], (L, D)),
        'w_router': nrm(ks[24], (L, D, N_EXPERTS), D ** -0.5),
        'b_router': nrm(ks[25], (L, N_EXPERTS), 0.01),
        'w_gu': nrm(ks[26], (L, N_EXPERTS, D, 2 * D_FF), D ** -0.5),
        'b_gu': nrm(ks[27], (L, N_EXPERTS, 2 * D_FF), 0.01),
        'w_dn': nrm(ks[28], (L, N_EXPERTS, D_FF, D), D_FF ** -0.5),
        'b_dn': nrm(ks[29], (L, N_EXPERTS, D), 0.01),
        'g_final': gain(ks[30], (D,)),
    }


def reference(x, mem, g_mix, w_in, b_f, lam_re, lam_im, log_step, b_re, b_im, c_re, c_im,
              d_skip, w_glu, b_glu, g_attn_out, g_ssm_out, w_out, g_cross, g_mem, w_cq, w_ckv,
              w_co, g_moe, w_router, b_router, w_gu, b_gu, w_dn, b_dn, g_final):
    h = x
    s1 = SSM_WIDTH
    s2 = s1 + ATTN_WIDTH
    s3 = s2 + ATTN_WIDTH
    s4 = s3 + ATTN_WIDTH
    for layer in range(DEPTH):
        n = rmsnorm(h, g_mix[layer])
        u, q, k, v, fl = jnp.split(n @ w_in[layer], [s1, s2, s3, s4], axis=-1)
        att = fox_attention(q, k, v, fl + b_f[layer])
        ssm = s5_mixer(u, lam_re[layer], lam_im[layer], log_step[layer], b_re[layer],
                       b_im[layer], c_re[layer], c_im[layer], d_skip[layer], w_glu[layer],
                       b_glu[layer])
        mixed = jnp.concatenate([rmsnorm(att, g_attn_out[layer]),
                                 rmsnorm(ssm, g_ssm_out[layer])], axis=-1)
        h = h + mixed @ w_out[layer]
        h = h + cross_attention(rmsnorm(h, g_cross[layer]), rmsnorm(mem, g_mem[layer]),
                                w_cq[layer], w_ckv[layer], w_co[layer])
        h = h + moe_ffn(rmsnorm(h, g_moe[layer]), w_router[layer], b_router[layer],
                        w_gu[layer], b_gu[layer], w_dn[layer], b_dn[layer])
    return rmsnorm(h, g_final)
```

```python
import functools

import jax
import jax.numpy as jnp
from jax import lax
from jax.experimental import pallas as pl
from jax.experimental.pallas import tpu as pltpu

F32 = jnp.float32
BF16 = jnp.bfloat16
I32 = jnp.int32

RMS_EPS = 1e-5
SSM_GROUP = 16
ATTN_HEAD_DIM = 128
CROSS_HEADS = 4
TOP_K = 4
SWIGLU_ALPHA = 1.702
SWIGLU_LIMIT = 7.0

LANES = 128
S5_CHUNK = 32
MOE_SUB = 256
MOE_SUBS = 4
MOE_FTILE = 256
NEG_BIG = -1e30
VMEM_LIMIT = 56 * 1024 * 1024

HIGHEST = lax.Precision.HIGHEST


def _cparams(sem, vmem=None):
    return pltpu.CompilerParams(dimension_semantics=sem, vmem_limit_bytes=vmem)


def _rms(x, g):
    return x * lax.rsqrt(jnp.mean(x * x, axis=-1, keepdims=True) + RMS_EPS) * g


def _dot(a, b, precision=None):
    return jnp.dot(a, b, preferred_element_type=F32, precision=precision)


def _dot_t(a, b):
    return lax.dot_general(a, b, (((1,), (1,)), ((), ())), preferred_element_type=F32)


def _norm_matmul_kernel(x_ref, g_ref, w_ref, o_ref, xn_ref, *, precision):
    @pl.when(pl.program_id(1) == 0)
    def _():
        xn_ref[...] = _rms(x_ref[...], g_ref[...]).astype(xn_ref.dtype)

    o_ref[...] = _dot(xn_ref[...], w_ref[...], precision).astype(o_ref.dtype)


def _norm_matmul(x, g, w, out_dtype, tm, tn, precision=None):
    n, k = x.shape
    m = w.shape[1]
    tm, tn = min(tm, n), min(tn, m)
    return pl.pallas_call(
        functools.partial(_norm_matmul_kernel, precision=precision),
        grid=(n // tm, m // tn),
        in_specs=[pl.BlockSpec((tm, k), lambda i, j: (i, 0)),
                  pl.BlockSpec((1, k), lambda i, j: (0, 0)),
                  pl.BlockSpec((k, tn), lambda i, j: (0, j))],
        out_specs=pl.BlockSpec((tm, tn), lambda i, j: (i, j)),
        out_shape=jax.ShapeDtypeStruct((n, m), out_dtype),
        scratch_shapes=[pltpu.VMEM((tm, k), w.dtype)],
        compiler_params=_cparams(("parallel", "arbitrary"), VMEM_LIMIT),
        name="norm_matmul",
    )(x, g.reshape(1, k), w)


def _fgate_cumsum_kernel(f_ref, b_ref, o_ref):
    z = f_ref[...] + b_ref[...]
    x = jnp.minimum(z, 0.0) - jnp.log(1.0 + jnp.exp(-jnp.abs(z)))
    lane = lax.broadcasted_iota(I32, x.shape, 1)
    d = 1
    while d < x.shape[1]:
        x = x + jnp.where(lane >= d, pltpu.roll(x, d, axis=1), 0.0)
        d *= 2
    o_ref[...] = x


def _fgate_cumsum(f_rows, bias_rows):
    return pl.pallas_call(
        _fgate_cumsum_kernel,
        out_shape=jax.ShapeDtypeStruct(f_rows.shape, F32),
        name="fgate_cumsum",
    )(f_rows, bias_rows)


def _fox_kernel(q_ref, k_ref, v_ref, cq_ref, ck_ref, o_ref, m_sc, l_sc, acc_sc, cq_sc, *, scale, tq):
    h = pl.program_id(1)
    qi = pl.program_id(2)
    ki = pl.program_id(3)

    @pl.when(ki == 0)
    def _():
        m_sc[...] = jnp.full(m_sc.shape, -jnp.inf, F32)
        l_sc[...] = jnp.zeros(l_sc.shape, F32)
        acc_sc[...] = jnp.zeros(acc_sc.shape, F32)
        cq = cq_ref[...]
        lane = lax.broadcasted_iota(I32, cq.shape, 1)
        cq_sc[...] = jnp.sum(jnp.where(lane == h, cq, 0.0), axis=1, keepdims=True)

    def step(masked):
        s = _dot_t(q_ref[...], k_ref[...]) * scale
        s = s + cq_sc[...] - ck_ref[0]
        if masked:
            row = lax.broadcasted_iota(I32, s.shape, 0)
            col = lax.broadcasted_iota(I32, s.shape, 1)
            s = jnp.where(col <= row, s, -jnp.inf)
        m_prev = m_sc[...]
        m_new = jnp.maximum(m_prev, jnp.max(s, axis=1, keepdims=True))
        alpha = jnp.exp(m_prev - m_new)
        p = jnp.exp(s - m_new)
        l_sc[...] = alpha * l_sc[...] + jnp.sum(p, axis=1, keepdims=True)
        acc_sc[...] = alpha * acc_sc[...] + _dot(p.astype(BF16), v_ref[...])
        m_sc[...] = m_new

    @pl.when(ki < qi)
    def _():
        step(False)

    @pl.when(ki == qi)
    def _():
        step(True)
        o_ref[...] = (acc_sc[...] / l_sc[...]).astype(o_ref.dtype)


def _fox_attention(proj, cum_col, cum_row, *, batch, length, heads, q_col, k_col, v_col, tq):
    dh = ATTN_HEAD_DIM
    tq = min(tq, length)
    nq = length // tq
    n = batch * length
    kernel = functools.partial(_fox_kernel, scale=dh ** -0.5, tq=tq)
    return pl.pallas_call(
        kernel,
        grid=(batch, heads, nq, nq),
        in_specs=[
            pl.BlockSpec((tq, dh), lambda b, h, qi, ki: (b * nq + qi, q_col + h)),
            pl.BlockSpec((tq, dh), lambda b, h, qi, ki: (b * nq + jnp.minimum(ki, qi), k_col + h)),
            pl.BlockSpec((tq, dh), lambda b, h, qi, ki: (b * nq + jnp.minimum(ki, qi), v_col + h)),
            pl.BlockSpec((tq, heads), lambda b, h, qi, ki: (b * nq + qi, 0)),
            pl.BlockSpec((1, 1, tq), lambda b, h, qi, ki: (b * heads + h, 0, jnp.minimum(ki, qi))),
        ],
        out_specs=pl.BlockSpec((tq, dh), lambda b, h, qi, ki: (b * nq + qi, h)),
        out_shape=jax.ShapeDtypeStruct((n, heads * dh), F32),
        scratch_shapes=[pltpu.VMEM((tq, 1), F32), pltpu.VMEM((tq, 1), F32),
                        pltpu.VMEM((tq, dh), F32), pltpu.VMEM((tq, 1), F32)],
        compiler_params=_cparams(("parallel", "parallel", "parallel", "arbitrary"), VMEM_LIMIT),
        name="fox_attention",
    )(proj, proj, proj, cum_col, cum_row)


def _s5_taps_kernel(wk_ref, cc_ref, o_ref):
    o_ref[0] = _dot(wk_ref[0], cc_ref[0], HIGHEST)


def _s5_kernel_taps(wk, cc):
    g, th, p2 = wk.shape
    h = cc.shape[2]
    return pl.pallas_call(
        _s5_taps_kernel,
        grid=(g,),
        in_specs=[pl.BlockSpec((1, th, p2), lambda i: (i, 0, 0)),
                  pl.BlockSpec((1, p2, h), lambda i: (i, 0, 0))],
        out_specs=pl.BlockSpec((1, th, h), lambda i: (i, 0, 0)),
        out_shape=jax.ShapeDtypeStruct((g, th, h), F32),
        compiler_params=_cparams(("parallel",)),
        name="s5_taps",
    )(wk, cc)


def _s5_mixer_kernel(u_ref, mi_ref, ms_ref, mo_ref, a1_ref, a2_ref, d_ref, y_ref, *, nlev, chunks, p):
    u = u_ref[0]
    s = _dot(u, ms_ref[0])
    row = lax.broadcasted_iota(I32, s.shape, 0) % chunks
    for j in range(nlev):
        d = 1 << j
        sh = jnp.where(row >= d, pltpu.roll(s, d, axis=0), 0.0)
        sw = pltpu.roll(sh, p, axis=1)
        s = s + a1_ref[0, j:j + 1, :] * sh + a2_ref[0, j:j + 1, :] * sw
    sp = jnp.where(row >= 1, pltpu.roll(s, 1, axis=0), 0.0)
    y = _dot(u, mi_ref[0]) + _dot(sp.astype(BF16), mo_ref[0]) + d_ref[0] * u.astype(F32)
    y_ref[0] = y


def _s5_mixer(ug, mi, ms, mo, a1, a2, dsk, *, chunks):
    g, r, th = ug.shape
    p2 = ms.shape[2]
    nlev = a1.shape[1]
    kernel = functools.partial(_s5_mixer_kernel, nlev=nlev, chunks=chunks, p=p2 // 2)
    return pl.pallas_call(
        kernel,
        grid=(g,),
        in_specs=[pl.BlockSpec((1, r, th), lambda i: (i, 0, 0)),
                  pl.BlockSpec((1, th, th), lambda i: (i, 0, 0)),
                  pl.BlockSpec((1, th, p2), lambda i: (i, 0, 0)),
                  pl.BlockSpec((1, p2, th), lambda i: (i, 0, 0)),
                  pl.BlockSpec((1, nlev, p2), lambda i: (i, 0, 0)),
                  pl.BlockSpec((1, nlev, p2), lambda i: (i, 0, 0)),
                  pl.BlockSpec((1, 1, th), lambda i: (i, 0, 0))],
        out_specs=pl.BlockSpec((1, r, th), lambda i: (i, 0, 0)),
        out_shape=jax.ShapeDtypeStruct((g, r, th), F32),
        compiler_params=_cparams(("parallel",), VMEM_LIMIT),
        name="s5_mixer",
    )(ug, mi, ms, mo, a1, a2, dsk)


def _s5_params(lam_re, lam_im, log_step, b_re, b_im, c_re, c_im, d_skip, t, chunks):
    g, p = lam_re.shape
    h = SSM_GROUP
    lr, li = lam_re.astype(F32), lam_im.astype(F32)
    dt = jnp.exp(log_step.astype(F32))[:, None]
    mag = jnp.exp(lr * dt)
    lb_re, lb_im = mag * jnp.cos(li * dt), mag * jnp.sin(li * dt)
    den = lr * lr + li * li
    nr, ni = lb_re - 1.0, lb_im
    coef_re = (nr * lr + ni * li) / den
    coef_im = (ni * lr - nr * li) / den
    br, bi = b_re.astype(F32), b_im.astype(F32)
    bb_re = coef_re[..., None] * br - coef_im[..., None] * bi
    bb_im = coef_re[..., None] * bi + coef_im[..., None] * br

    def power(k):
        return jnp.exp(lr * dt * k) * jnp.cos(li * dt * k), jnp.exp(lr * dt * k) * jnp.sin(li * dt * k)

    kk = jnp.arange(t + 1, dtype=F32)[:, None, None]
    pw_re, pw_im = power(kk)
    wb_re = pw_re[:t, ..., None] * bb_re - pw_im[:t, ..., None] * bb_im
    wb_im = pw_re[:t, ..., None] * bb_im + pw_im[:t, ..., None] * bb_re
    wk = jnp.concatenate([wb_re, wb_im], axis=2).transpose(1, 0, 3, 2)
    ms = wk[:, ::-1].reshape(g, t * h, 2 * p)
    cre, cim = c_re.astype(F32), c_im.astype(F32)
    cc = jnp.concatenate([cre.transpose(0, 2, 1), -cim.transpose(0, 2, 1)], axis=1)
    taps = _s5_kernel_taps(wk.reshape(g, t * h, 2 * p), cc)
    kt = taps.reshape(g, t, h, h).transpose(0, 2, 1, 3).reshape(g, h, t * h)
    kpad = jnp.concatenate([jnp.zeros_like(kt), kt], axis=2)
    mi = jnp.stack([kpad[:, :, (t - s) * h:(2 * t - s) * h] for s in range(t)], axis=1)
    mi = mi.reshape(g, t * h, t * h)
    cre_t, cim_t = cre.transpose(0, 2, 1)[:, :, None, :], cim.transpose(0, 2, 1)[:, :, None, :]
    are = pw_re[1:].transpose(1, 2, 0)[..., None]
    aim = pw_im[1:].transpose(1, 2, 0)[..., None]
    mo_re = cre_t * are - cim_t * aim
    mo_im = -(cre_t * aim + cim_t * are)
    mo = jnp.concatenate([mo_re, mo_im], axis=1).reshape(g, 2 * p, t * h)
    nlev = max(1, (chunks - 1).bit_length())
    lv = (float(t) * (2.0 ** jnp.arange(nlev, dtype=F32)))[None, :, None]
    ar, ai = power_levels(lr, li, dt, lv)
    a1 = jnp.concatenate([ar, ar], axis=2)
    a2 = jnp.concatenate([-ai, ai], axis=2)
    dsk = jnp.tile(d_skip.astype(F32).reshape(g, 1, h), (1, 1, t))
    return mi.astype(BF16), ms.astype(BF16), mo.astype(BF16), a1, a2, dsk


def power_levels(lr, li, dt, lv):
    e = (lr * dt)[:, None, :] * lv
    w = (li * dt)[:, None, :] * lv
    return jnp.exp(e) * jnp.cos(w), jnp.exp(e) * jnp.sin(w)


def _mix_out_kernel(y_ref, att_ref, x_ref, wglu_ref, bglu_ref, gs_ref, ga_ref, woa_ref, wos_ref, o_ref):
    y = jax.nn.gelu(y_ref[...])
    z = _dot(y.astype(BF16), wglu_ref[...]) + bglu_ref[...]
    ssm = y * jax.nn.sigmoid(z)
    na = _rms(att_ref[...], ga_ref[...]).astype(BF16)
    ns = _rms(ssm, gs_ref[...]).astype(BF16)
    o_ref[...] = x_ref[...] + _dot(na, woa_ref[...]) + _dot(ns, wos_ref[...])


def _mix_out(ypre, att, x, w_glu, b_glu, g_ssm, g_att, wo_a, wo_s, tm):
    n, d = x.shape
    sw, aw = ypre.shape[1], att.shape[1]
    tm = min(tm, n)
    row = lambda i: (i, 0)
    fix = lambda i: (0, 0)
    once = dict(pipeline_mode=pl.Buffered(1))
    return pl.pallas_call(
        _mix_out_kernel,
        grid=(n // tm,),
        in_specs=[pl.BlockSpec((tm, sw), row), pl.BlockSpec((tm, aw), row), pl.BlockSpec((tm, d), row),
                  pl.BlockSpec((sw, sw), fix, **once), pl.BlockSpec((1, sw), fix),
                  pl.BlockSpec((1, sw), fix), pl.BlockSpec((1, aw), fix),
                  pl.BlockSpec((aw, d), fix, **once), pl.BlockSpec((sw, d), fix, **once)],
        out_specs=pl.BlockSpec((tm, d), row),
        out_shape=jax.ShapeDtypeStruct((n, d), F32),
        compiler_params=_cparams(("parallel",), VMEM_LIMIT),
        name="mix_out",
    )(ypre, att, x, w_glu, b_glu.reshape(1, sw), g_ssm.reshape(1, sw), g_att.reshape(1, aw), wo_a, wo_s)


def _cross_kernel(h_ref, g_ref, wq_ref, kv_ref, wo_ref, o_ref, *, heads):
    h = h_ref[...]
    d = h.shape[1]
    dh = d // heads
    q = _dot(_rms(h, g_ref[...]).astype(BF16), wq_ref[...]).astype(BF16)
    out = h
    for hd in range(heads):
        lo = hd * dh
        s = _dot_t(q[:, lo:lo + dh], kv_ref[:, lo:lo + dh]) * dh ** -0.5
        e = jnp.exp(s - jnp.max(s, axis=1, keepdims=True))
        p = e / jnp.sum(e, axis=1, keepdims=True)
        oh = _dot(p.astype(BF16), kv_ref[:, d + lo:d + lo + dh])
        out = out + _dot(oh.astype(BF16), wo_ref[lo:lo + dh, :])
    o_ref[...] = out


def _cross_attention(h, g, wq, kv, wo, *, batch, mem_len, tm):
    n, d = h.shape
    tm = min(tm, n // batch)
    tiles = n // batch // tm
    fix = lambda i: (0, 0)
    once = dict(pipeline_mode=pl.Buffered(1))
    return pl.pallas_call(
        functools.partial(_cross_kernel, heads=CROSS_HEADS),
        grid=(n // tm,),
        in_specs=[pl.BlockSpec((tm, d), lambda i: (i, 0)),
                  pl.BlockSpec((1, d), fix),
                  pl.BlockSpec((d, d), fix, **once),
                  pl.BlockSpec((mem_len, 2 * d), lambda i: (i // tiles, 0)),
                  pl.BlockSpec((d, d), fix, **once)],
        out_specs=pl.BlockSpec((tm, d), lambda i: (i, 0)),
        out_shape=jax.ShapeDtypeStruct((n, d), F32),
        compiler_params=_cparams(("parallel",), VMEM_LIMIT),
        name="cross_attention",
    )(h, g.reshape(1, d), wq, kv, wo)


def _router_kernel(h_ref, g_ref, wr_ref, br_ref, xm_ref, idx_ref, gate_ref, rank_ref, cnt_ref, carry_sc):
    @pl.when(pl.program_id(0) == 0)
    def _():
        carry_sc[...] = jnp.zeros(carry_sc.shape, F32)

    hm = _rms(h_ref[...], g_ref[...])
    xm_ref[...] = hm
    logits = _dot(hm, wr_ref[...], HIGHEST) + br_ref[...]
    tm = logits.shape[0]
    lane = lax.broadcasted_iota(I32, logits.shape, 1).astype(F32)
    work = logits
    vals, idxs = [], []
    for _ in range(TOP_K):
        m = jnp.max(work, axis=1, keepdims=True)
        am = jnp.min(jnp.where(work == m, lane, float(LANES)), axis=1, keepdims=True)
        vals.append(m)
        idxs.append(am)
        work = jnp.where(lane == am, NEG_BIG, work)
    es = [jnp.exp(v - vals[0]) for v in vals]
    den = es[0] + es[1] + es[2] + es[3]
    sel = jnp.zeros(logits.shape, F32)
    for am in idxs:
        sel = sel + (lane == am).astype(F32)
    r = lax.broadcasted_iota(I32, (tm, tm), 0)
    c = lax.broadcasted_iota(I32, (tm, tm), 1)
    before = (c < r).astype(BF16)
    rank_all = carry_sc[0:1, :] + _dot(before, sel.astype(BF16))
    idx_o = jnp.zeros(logits.shape, F32)
    gate_o = jnp.zeros(logits.shape, F32)
    rank_o = jnp.zeros(logits.shape, F32)
    for j in range(TOP_K):
        slot = lane == float(j)
        rk = jnp.sum(jnp.where(lane == idxs[j], rank_all, 0.0), axis=1, keepdims=True)
        idx_o = jnp.where(slot, idxs[j], idx_o)
        gate_o = jnp.where(slot, es[j] / den, gate_o)
        rank_o = jnp.where(slot, rk, rank_o)
    idx_ref[...] = idx_o.astype(I32)
    gate_ref[...] = gate_o
    rank_ref[...] = rank_o.astype(I32)
    carry_sc[...] = carry_sc[...] + jnp.sum(sel, axis=0, keepdims=True)
    cnt_ref[...] = carry_sc[...].astype(I32)


def _router(h, g, wr, br, tm):
    n, d = h.shape
    tm = min(tm, n)
    row = lambda i: (i, 0)
    fix = lambda i: (0, 0)
    return pl.pallas_call(
        _router_kernel,
        grid=(n // tm,),
        in_specs=[pl.BlockSpec((tm, d), row), pl.BlockSpec((1, d), fix),
                  pl.BlockSpec((d, LANES), fix), pl.BlockSpec((1, LANES), fix)],
        out_specs=[pl.BlockSpec((tm, d), row), pl.BlockSpec((tm, LANES), row),
                   pl.BlockSpec((tm, LANES), row), pl.BlockSpec((tm, LANES), row),
                   pl.BlockSpec((8, LANES), fix)],
        out_shape=[jax.ShapeDtypeStruct((n, d), F32), jax.ShapeDtypeStruct((n, LANES), I32),
                   jax.ShapeDtypeStruct((n, LANES), F32), jax.ShapeDtypeStruct((n, LANES), I32),
                   jax.ShapeDtypeStruct((8, LANES), I32)],
        scratch_shapes=[pltpu.VMEM((8, LANES), F32)],
        compiler_params=_cparams(("arbitrary",), VMEM_LIMIT),
        name="router",
    )(h, g.reshape(1, d), wr, br)


def _row_copy(src_hbm, row, dst, slot, sem):
    return pltpu.make_async_copy(src_hbm.at[pl.ds(row, 1), :], dst.at[pl.ds(slot, 1), :], sem)


def _slab_copy(src_hbm, dst, slot, rows, sem):
    return pltpu.make_async_copy(src_hbm.at[pl.ds(0, rows), :], dst.at[pl.ds(slot, rows), :], sem)


def _expert_kernel(sbe_ref, sbn_ref, nused_ref, tok_ref, xm_hbm, wg_ref, wl_ref, wd_ref, bg_ref, bl_ref,
                   bd_ref, o_ref, xg_ref, xb_ref, wgb_ref, wlb_ref, wdb_ref, sem):
    i = pl.program_id(0)
    j = pl.program_id(1)
    nrows = sbn_ref[i]
    nsub = (nrows + MOE_SUB - 1) // MOE_SUB

    @pl.when(jnp.logical_and(j == 0, nrows > 0))
    def _():
        def issue(r, c):
            _row_copy(xm_hbm, tok_ref[0, 0, r], xg_ref, r, sem).start()
            return c

        lax.fori_loop(0, nsub * MOE_SUB, issue, 0)
        for s in range(MOE_SUBS):
            @pl.when(s < nsub)
            def _():
                _slab_copy(xm_hbm, xg_ref, s * MOE_SUB, MOE_SUB, sem).wait()

        for s in range(MOE_SUBS):
            rows = pl.ds(s * MOE_SUB, MOE_SUB)

            @pl.when(s < nsub)
            def _():
                xb_ref[rows, :] = xg_ref[rows, :].astype(BF16)

            @pl.when(s >= nsub)
            def _():
                o_ref[rows, :] = jnp.zeros((MOE_SUB, o_ref.shape[1]), F32)

    @pl.when(nrows > 0)
    def _():
        wgb_ref[...] = wg_ref[...].astype(BF16)
        wlb_ref[...] = wl_ref[...].astype(BF16)
        wdb_ref[...] = wd_ref[...].astype(BF16)
        for s in range(MOE_SUBS):
            rows = pl.ds(s * MOE_SUB, MOE_SUB)

            @pl.when(s < nsub)
            def _():
                x = xb_ref[rows, :]
                g = jnp.minimum(_dot(x, wgb_ref[...]) + bg_ref[...], SWIGLU_LIMIT)
                lin = jnp.clip(_dot(x, wlb_ref[...]) + bl_ref[...], -SWIGLU_LIMIT, SWIGLU_LIMIT)
                act = g * jax.nn.sigmoid(SWIGLU_ALPHA * g) * (lin + 1.0)
                y = _dot(act.astype(BF16), wdb_ref[...])

                @pl.when(j == 0)
                def _():
                    o_ref[rows, :] = y + bd_ref[...]

                @pl.when(j > 0)
                def _():
                    o_ref[rows, :] = o_ref[rows, :] + y


def _experts(xm, row_tok, sb_e, sb_n, n_used, w_gu, b_gu, w_dn, b_dn):
    n_sb = sb_e.shape[0]
    sb_rows = MOE_SUBS * MOE_SUB
    e, d, f2 = w_gu.shape
    f = f2 // 2
    tf = min(MOE_FTILE, f)
    nf = f // tf

    def jj(i, j, nused):
        return jnp.where(i < nused[0], j, nf - 1)

    grid_spec = pltpu.PrefetchScalarGridSpec(
        num_scalar_prefetch=3,
        grid=(n_sb, nf),
        in_specs=[
            pl.BlockSpec((1, 1, sb_rows), lambda i, j, sbe, sbn, nu: (i, 0, 0), memory_space=pltpu.SMEM),
            pl.BlockSpec(memory_space=pl.ANY),
            pl.BlockSpec((None, d, tf), lambda i, j, sbe, sbn, nu: (sbe[i], 0, jj(i, j, nu))),
            pl.BlockSpec((None, d, tf), lambda i, j, sbe, sbn, nu: (sbe[i], 0, nf + jj(i, j, nu))),
            pl.BlockSpec((None, tf, d), lambda i, j, sbe, sbn, nu: (sbe[i], jj(i, j, nu), 0)),
            pl.BlockSpec((None, 1, tf), lambda i, j, sbe, sbn, nu: (sbe[i], 0, jj(i, j, nu))),
            pl.BlockSpec((None, 1, tf), lambda i, j, sbe, sbn, nu: (sbe[i], 0, nf + jj(i, j, nu))),
            pl.BlockSpec((None, 1, d), lambda i, j, sbe, sbn, nu: (sbe[i], 0, 0)),
        ],
        out_specs=pl.BlockSpec((sb_rows, d), lambda i, j, sbe, sbn, nu: (jnp.where(i < nu[0], i, n_sb), 0)),
        scratch_shapes=[pltpu.VMEM((sb_rows, d), F32), pltpu.VMEM((sb_rows, d), BF16),
                        pltpu.VMEM((d, tf), BF16), pltpu.VMEM((d, tf), BF16), pltpu.VMEM((tf, d), BF16),
                        pltpu.SemaphoreType.DMA],
    )
    return pl.pallas_call(
        _expert_kernel,
        grid_spec=grid_spec,
        out_shape=jax.ShapeDtypeStruct(((n_sb + 1) * sb_rows, d), F32),
        compiler_params=_cparams(("arbitrary", "arbitrary"), VMEM_LIMIT),
        name="experts",
    )(sb_e, sb_n, n_used, row_tok.reshape(n_sb, 1, sb_rows), xm, w_gu, w_gu, w_dn,
      b_gu.reshape(e, 1, f2), b_gu.reshape(e, 1, f2), b_dn.reshape(e, 1, d))


def _combine_kernel(dest_ref, gate_ref, h_ref, g_ref, y_hbm, o_ref, buf_ref, sem):
    tm = h_ref.shape[0]

    def issue(t, c):
        for k in range(TOP_K):
            _row_copy(y_hbm, dest_ref[0, 0, t * TOP_K + k], buf_ref.at[k], t, sem).start()
        return c

    lax.fori_loop(0, tm, issue, 0)
    acc = h_ref[...]
    gates = gate_ref[...]
    for k in range(TOP_K):
        _slab_copy(y_hbm, buf_ref.at[k], 0, tm, sem).wait()
    for k in range(TOP_K):
        acc = acc + gates[:, k:k + 1] * buf_ref[k]
    o_ref[...] = _rms(acc, g_ref[...])


def _combine(dest, gates, h, g, yrows, tm):
    n, d = h.shape
    tm = min(tm, n)
    return pl.pallas_call(
        _combine_kernel,
        grid=(n // tm,),
        in_specs=[pl.BlockSpec((1, 1, tm * TOP_K), lambda i: (i, 0, 0), memory_space=pltpu.SMEM),
                  pl.BlockSpec((tm, LANES), lambda i: (i, 0)),
                  pl.BlockSpec((tm, d), lambda i: (i, 0)),
                  pl.BlockSpec((1, d), lambda i: (0, 0)),
                  pl.BlockSpec(memory_space=pl.ANY)],
        out_specs=pl.BlockSpec((tm, d), lambda i: (i, 0)),
        out_shape=jax.ShapeDtypeStruct((n, d), F32),
        scratch_shapes=[pltpu.VMEM((TOP_K, tm, d), F32), pltpu.SemaphoreType.DMA],
        compiler_params=_cparams(("arbitrary",), VMEM_LIMIT),
        name="combine",
    )(dest.reshape(n // tm, 1, tm * TOP_K), gates, h, g.reshape(1, d), yrows)


def _layer(h, mem2d, p, *, batch, length):
    n, d = h.shape
    g_cnt, p_state = p["lam_re"].shape
    sw = g_cnt * SSM_GROUP
    aw = d - sw
    heads = aw // ATTN_HEAD_DIM
    assert p["w_in"].shape[1] == sw + 3 * aw + heads and sw % LANES == 0 and 2 * p_state == LANES

    w_in = p["w_in"]
    cols = sw + 3 * aw
    proj = _norm_matmul(h, p["g_mix"], w_in[:, :cols].astype(BF16), BF16, 512, 512)
    w_f = jnp.pad(w_in[:, cols:], ((0, 0), (0, LANES - heads)))
    flog = _norm_matmul(h, p["g_mix"], w_f, F32, 512, LANES, HIGHEST)

    f_rows = flog[:, :heads].reshape(batch, length, heads).transpose(0, 2, 1).reshape(batch * heads, length)
    bias_rows = jnp.tile(p["b_f"].astype(F32), batch).reshape(batch * heads, 1)
    cum_rows = _fgate_cumsum(f_rows, bias_rows)
    cum_col = cum_rows.reshape(batch, heads, length).transpose(0, 2, 1).reshape(n, heads)

    blk = lambda c: c // ATTN_HEAD_DIM
    att = _fox_attention(proj, cum_col, cum_rows.reshape(batch * heads, 1, length), batch=batch, length=length,
                         heads=heads, q_col=blk(sw), k_col=blk(sw + aw), v_col=blk(sw + 2 * aw), tq=512)

    t = min(S5_CHUNK, length)
    chunks = length // t
    mi, ms, mo, a1, a2, dsk = _s5_params(p["lam_re"], p["lam_im"], p["log_step"], p["b_re"], p["b_im"],
                                         p["c_re"], p["c_im"], p["d_skip"], t, chunks)
    ug = proj[:, :sw].reshape(batch, chunks, t, g_cnt, SSM_GROUP).transpose(3, 0, 1, 2, 4)
    ug = ug.reshape(g_cnt, batch * chunks, t * SSM_GROUP)
    yg = _s5_mixer(ug, mi, ms, mo, a1, a2, dsk, chunks=chunks)
    ypre = yg.reshape(g_cnt, batch, chunks, t, SSM_GROUP).transpose(1, 2, 3, 0, 4).reshape(n, sw)

    w_out = p["w_out"].astype(BF16)
    h = _mix_out(ypre, att, h, p["w_glu"].astype(BF16), p["b_glu"], p["g_ssm_out"], p["g_attn_out"],
                 w_out[:aw], w_out[aw:], 256)

    mem_len = mem2d.shape[0] // batch
    kv = _norm_matmul(mem2d, p["g_mem"], p["w_ckv"].astype(BF16), BF16, 512, 512)
    h = _cross_attention(h, p["g_cross"], p["w_cq"].astype(BF16), kv, p["w_co"].astype(BF16),
                         batch=batch, mem_len=mem_len, tm=256)

    dest, gates, yrows = _moe(h, p)
    return dest, gates, h, yrows


def _moe(h, p):
    n = h.shape[0]
    n_exp = p["w_router"].shape[1]
    wr = jnp.pad(p["w_router"].astype(F32), ((0, 0), (0, LANES - n_exp)))
    br = jnp.pad(p["b_router"].astype(F32), (0, LANES - n_exp), constant_values=NEG_BIG).reshape(1, LANES)
    xm, idx, gates, rank, cnt = _router(h, p["g_moe"], wr, br, 256)

    sb_rows = MOE_SUBS * MOE_SUB
    n_sb = (n * TOP_K) // sb_rows + n_exp
    counts = cnt[0, :n_exp]
    padded = (counts + sb_rows - 1) // sb_rows * sb_rows
    pend = jnp.cumsum(padded)
    pstart = pend - padded
    idx_k, rank_k = idx[:, :TOP_K], rank[:, :TOP_K]
    onehot = idx_k[..., None] == jnp.arange(n_exp, dtype=I32)
    dest = rank_k + jnp.sum(jnp.where(onehot, pstart, 0), axis=-1)
    tok = jnp.broadcast_to(jnp.arange(n, dtype=I32)[:, None], (n, TOP_K))
    row_tok = jnp.zeros((n_sb * sb_rows,), I32).at[dest.reshape(-1)].set(tok.reshape(-1), unique_indices=True)
    n_used = (pend[-1] // sb_rows).astype(I32)
    sb_start = jnp.arange(n_sb, dtype=I32) * sb_rows
    sb_e = jnp.clip(jnp.searchsorted(pend, sb_start, side="right"), 0, n_exp - 1).astype(I32)
    used = jnp.arange(n_sb, dtype=I32) < n_used
    sb_n = jnp.where(used, jnp.clip(counts[sb_e] - (sb_start - pstart[sb_e]), 0, sb_rows), 0).astype(I32)
    sb_e = jnp.where(used, sb_e, sb_e[jnp.maximum(n_used - 1, 0)])

    yrows = _experts(xm, row_tok, sb_e, sb_n, n_used.reshape(1), p["w_gu"], p["b_gu"], p["w_dn"], p["b_dn"])
    return dest, gates, yrows


def kernel(x, mem, g_mix, w_in, b_f, lam_re, lam_im, log_step, b_re, b_im, c_re, c_im, d_skip, w_glu, b_glu, g_attn_out, g_ssm_out, w_out, g_cross, g_mem, w_cq, w_ckv, w_co, g_moe, w_router, b_router, w_gu, b_gu, w_dn, b_dn, g_final):
    batch, length, d = x.shape
    depth = g_mix.shape[0]
    assert depth == 1, "the combine kernel applies the final norm, so exactly one layer is supported"
    names = ("g_mix", "w_in", "b_f", "lam_re", "lam_im", "log_step", "b_re", "b_im", "c_re", "c_im", "d_skip",
             "w_glu", "b_glu", "g_attn_out", "g_ssm_out", "w_out", "g_cross", "g_mem", "w_cq", "w_ckv", "w_co",
             "g_moe", "w_router", "b_router", "w_gu", "b_gu", "w_dn", "b_dn")
    vals = (g_mix, w_in, b_f, lam_re, lam_im, log_step, b_re, b_im, c_re, c_im, d_skip, w_glu, b_glu,
            g_attn_out, g_ssm_out, w_out, g_cross, g_mem, w_cq, w_ckv, w_co, g_moe, w_router, b_router,
            w_gu, b_gu, w_dn, b_dn)
    params = {k: v[0] for k, v in zip(names, vals)}
    h = x.reshape(batch * length, d)
    mem2d = mem.reshape(-1, d)
    dest, gates, h, yrows = _layer(h, mem2d, params, batch=batch, length=length)
    out = _combine(dest, gates, h, g_final, yrows, 256)
    return out.reshape(batch, length, d)
```

```python
import functools

import jax
import jax.numpy as jnp
from jax import lax
from jax.experimental import pallas as pl
from jax.experimental.pallas import tpu as pltpu

F32 = jnp.float32
BF16 = jnp.bfloat16
I32 = jnp.int32

RMS_EPS = 1e-5
SSM_GROUP = 16
ATTN_HEAD_DIM = 128
CROSS_HEADS = 4
TOP_K = 4
SWIGLU_ALPHA = 1.702
SWIGLU_LIMIT = 7.0

LANES = 128
LOG2E = 1.4426950408889634
ATT_BLOCK = 1024
ATT_SQ = 128
ATT_SK = 256
S5_CHUNK = 32
MOE_SUB = 256
MOE_SUBS = 4
MOE_FTILE = 256
MOE_NCHUNK = 512
ISSUE_UNROLL = 8
NEG_BIG = -1e30
VMEM_LIMIT = 56 * 1024 * 1024

HIGHEST = lax.Precision.HIGHEST


def _cparams(sem, vmem=None):
    return pltpu.CompilerParams(dimension_semantics=sem, vmem_limit_bytes=vmem)


def _rms(x, g):
    return x * lax.rsqrt(jnp.mean(x * x, axis=-1, keepdims=True) + RMS_EPS) * g


def _dot(a, b, precision=None):
    return jnp.dot(a, b, preferred_element_type=F32, precision=precision)


def _dot_t(a, b):
    return lax.dot_general(a, b, (((1,), (1,)), ((), ())), preferred_element_type=F32)


def _norm_matmul_kernel(x_ref, g_ref, w_ref, cs_ref, o_ref, xn_ref, *, precision):
    @pl.when(pl.program_id(1) == 0)
    def _():
        xn_ref[...] = _rms(x_ref[...], g_ref[...]).astype(xn_ref.dtype)

    o_ref[...] = (_dot(xn_ref[...], w_ref[...], precision) * cs_ref[...]).astype(o_ref.dtype)


def _norm_matmul(x, g, w, col_scale, out_dtype, tm, tn, precision=None):
    n, k = x.shape
    m = w.shape[1]
    tm, tn = min(tm, n), min(tn, m)
    return pl.pallas_call(
        functools.partial(_norm_matmul_kernel, precision=precision),
        grid=(n // tm, m // tn),
        in_specs=[pl.BlockSpec((tm, k), lambda i, j: (i, 0)),
                  pl.BlockSpec((1, k), lambda i, j: (0, 0)),
                  pl.BlockSpec((k, tn), lambda i, j: (0, j)),
                  pl.BlockSpec((1, tn), lambda i, j: (0, j))],
        out_specs=pl.BlockSpec((tm, tn), lambda i, j: (i, j)),
        out_shape=jax.ShapeDtypeStruct((n, m), out_dtype),
        scratch_shapes=[pltpu.VMEM((tm, k), w.dtype)],
        compiler_params=_cparams(("parallel", "arbitrary"), VMEM_LIMIT),
        name="norm_matmul",
    )(x, g.reshape(1, k), w, col_scale.reshape(1, m))


def _fgate_cumsum_kernel(f_ref, b_ref, o_ref):
    z = f_ref[...] + b_ref[...]
    x = jnp.minimum(z, 0.0) - jnp.log(1.0 + jnp.exp(-jnp.abs(z)))
    lane = lax.broadcasted_iota(I32, x.shape, 1)
    d = 1
    while d < x.shape[1]:
        x = x + jnp.where(lane >= d, pltpu.roll(x, d, axis=1), 0.0)
        d *= 2
    o_ref[...] = x * LOG2E


def _fgate_cumsum(f_rows, bias_rows):
    return pl.pallas_call(
        _fgate_cumsum_kernel,
        out_shape=jax.ShapeDtypeStruct(f_rows.shape, F32),
        name="fgate_cumsum",
    )(f_rows, bias_rows)


def _fox_kernel(qi_ref, ki_ref, q_ref, k_ref, v_ref, cq_ref, ck_ref, o_ref, m_sc, l_sc, acc_sc, cq_sc,
                *, blk, sq, sk):
    h = pl.program_id(1)
    t = pl.program_id(2)
    qi = qi_ref[t]
    ki = ki_ref[t]
    rep = sk // LANES

    @pl.when(ki == 0)
    def _():
        m_sc[...] = jnp.full(m_sc.shape, -jnp.inf, F32)
        l_sc[...] = jnp.zeros(l_sc.shape, F32)
        acc_sc[...] = jnp.zeros(acc_sc.shape, F32)
        cq = cq_ref[...]
        lane = lax.broadcasted_iota(I32, cq.shape, 1)
        col = jnp.sum(jnp.where(lane == h, cq, 0.0), axis=1, keepdims=True)
        cq_sc[...] = jnp.broadcast_to(col, cq_sc.shape)

    def block(diag):
        for qs in range(blk // sq):
            r0 = qs * sq
            rows = pl.ds(r0, sq)
            q = q_ref[rows, :]
            cq = jnp.concatenate([cq_sc[rows, :]] * rep, axis=1)
            m, l, acc = m_sc[rows, :], l_sc[rows, :], acc_sc[rows, :]
            nkc = (r0 + sq - 1) // sk + 1 if diag else blk // sk
            for kc in range(nkc):
                c0 = kc * sk
                cols = pl.ds(c0, sk)
                s = _dot_t(q, k_ref[cols, :]) + cq - ck_ref[0, :, cols]
                if diag and c0 + sk - 1 > r0:
                    row = lax.broadcasted_iota(I32, s.shape, 0) + r0
                    col = lax.broadcasted_iota(I32, s.shape, 1) + c0
                    s = jnp.where(col <= row, s, -jnp.inf)
                m_new = jnp.maximum(m, jnp.max(s, axis=1, keepdims=True))
                alpha = jnp.exp2(m - m_new)
                p = jnp.exp2(s - jnp.concatenate([m_new] * rep, axis=1))
                psum = p[:, :LANES]
                for c in range(1, rep):
                    psum = psum + p[:, c * LANES:(c + 1) * LANES]
                l = alpha * l + psum
                acc = alpha * acc + _dot(p.astype(BF16), v_ref[cols, :])
                m = m_new
            if diag:
                o_ref[rows, :] = (acc / jnp.sum(l, axis=1, keepdims=True)).astype(o_ref.dtype)
            else:
                m_sc[rows, :], l_sc[rows, :], acc_sc[rows, :] = m, l, acc

    @pl.when(ki < qi)
    def _():
        block(False)

    @pl.when(ki == qi)
    def _():
        block(True)


def _fox_attention(proj, cum_col, cum_row, *, batch, length, heads, q_col, k_col, v_col):
    dh = ATTN_HEAD_DIM
    blk = min(ATT_BLOCK, length)
    sq, sk = min(ATT_SQ, blk), min(ATT_SK, blk)
    nq = length // blk
    n = batch * length
    pairs = [(a, b) for a in range(nq) for b in range(a + 1)]
    qi_tab = jnp.asarray([a for a, _ in pairs], I32)
    ki_tab = jnp.asarray([b for _, b in pairs], I32)
    kernel = functools.partial(_fox_kernel, blk=blk, sq=sq, sk=sk)
    grid_spec = pltpu.PrefetchScalarGridSpec(
        num_scalar_prefetch=2,
        grid=(batch, heads, len(pairs)),
        in_specs=[
            pl.BlockSpec((blk, dh), lambda b, h, t, qt, kt: (b * nq + qt[t], q_col + h)),
            pl.BlockSpec((blk, dh), lambda b, h, t, qt, kt: (b * nq + kt[t], k_col + h)),
            pl.BlockSpec((blk, dh), lambda b, h, t, qt, kt: (b * nq + kt[t], v_col + h)),
            pl.BlockSpec((blk, heads), lambda b, h, t, qt, kt: (b * nq + qt[t], 0)),
            pl.BlockSpec((1, 1, blk), lambda b, h, t, qt, kt: (b * heads + h, 0, kt[t])),
        ],
        out_specs=pl.BlockSpec((blk, dh), lambda b, h, t, qt, kt: (b * nq + qt[t], h)),
        scratch_shapes=[pltpu.VMEM((blk, LANES), F32), pltpu.VMEM((blk, LANES), F32),
                        pltpu.VMEM((blk, dh), F32), pltpu.VMEM((blk, LANES), F32)],
    )
    return pl.pallas_call(
        kernel,
        grid_spec=grid_spec,
        out_shape=jax.ShapeDtypeStruct((n, heads * dh), F32),
        compiler_params=_cparams(("parallel", "parallel", "arbitrary"), VMEM_LIMIT),
        name="fox_attention",
    )(qi_tab, ki_tab, proj, proj, proj, cum_col, cum_row)


def _s5_taps_kernel(wk_ref, cc_ref, o_ref):
    o_ref[0] = _dot(wk_ref[0], cc_ref[0], HIGHEST)


def _s5_kernel_taps(wk, cc):
    g, th, p2 = wk.shape
    h = cc.shape[2]
    return pl.pallas_call(
        _s5_taps_kernel,
        grid=(g,),
        in_specs=[pl.BlockSpec((1, th, p2), lambda i: (i, 0, 0)),
                  pl.BlockSpec((1, p2, h), lambda i: (i, 0, 0))],
        out_specs=pl.BlockSpec((1, th, h), lambda i: (i, 0, 0)),
        out_shape=jax.ShapeDtypeStruct((g, th, h), F32),
        compiler_params=_cparams(("parallel",)),
        name="s5_taps",
    )(wk, cc)


def _s5_mixer_kernel(u_ref, mi_ref, ms_ref, mo_ref, a1_ref, a2_ref, d_ref, y_ref, *, nlev, chunks, p):
    u = u_ref[0]
    s = _dot(u, ms_ref[0])
    row = lax.broadcasted_iota(I32, s.shape, 0) % chunks
    for j in range(nlev):
        d = 1 << j
        sh = jnp.where(row >= d, pltpu.roll(s, d, axis=0), 0.0)
        sw = pltpu.roll(sh, p, axis=1)
        s = s + a1_ref[0, j:j + 1, :] * sh + a2_ref[0, j:j + 1, :] * sw
    sp = jnp.where(row >= 1, pltpu.roll(s, 1, axis=0), 0.0)
    y = _dot(u, mi_ref[0]) + _dot(sp.astype(BF16), mo_ref[0]) + d_ref[0] * u.astype(F32)
    y_ref[0] = y


def _s5_mixer(ug, mi, ms, mo, a1, a2, dsk, *, chunks):
    g, r, th = ug.shape
    p2 = ms.shape[2]
    nlev = a1.shape[1]
    kernel = functools.partial(_s5_mixer_kernel, nlev=nlev, chunks=chunks, p=p2 // 2)
    return pl.pallas_call(
        kernel,
        grid=(g,),
        in_specs=[pl.BlockSpec((1, r, th), lambda i: (i, 0, 0)),
                  pl.BlockSpec((1, th, th), lambda i: (i, 0, 0)),
                  pl.BlockSpec((1, th, p2), lambda i: (i, 0, 0)),
                  pl.BlockSpec((1, p2, th), lambda i: (i, 0, 0)),
                  pl.BlockSpec((1, nlev, p2), lambda i: (i, 0, 0)),
                  pl.BlockSpec((1, nlev, p2), lambda i: (i, 0, 0)),
                  pl.BlockSpec((1, 1, th), lambda i: (i, 0, 0))],
        out_specs=pl.BlockSpec((1, r, th), lambda i: (i, 0, 0)),
        out_shape=jax.ShapeDtypeStruct((g, r, th), F32),
        compiler_params=_cparams(("parallel",), VMEM_LIMIT),
        name="s5_mixer",
    )(ug, mi, ms, mo, a1, a2, dsk)


def _s5_params(lam_re, lam_im, log_step, b_re, b_im, c_re, c_im, d_skip, t, chunks):
    g, p = lam_re.shape
    h = SSM_GROUP
    lr, li = lam_re.astype(F32), lam_im.astype(F32)
    dt = jnp.exp(log_step.astype(F32))[:, None]
    mag = jnp.exp(lr * dt)
    lb_re, lb_im = mag * jnp.cos(li * dt), mag * jnp.sin(li * dt)
    den = lr * lr + li * li
    nr, ni = lb_re - 1.0, lb_im
    coef_re = (nr * lr + ni * li) / den
    coef_im = (ni * lr - nr * li) / den
    br, bi = b_re.astype(F32), b_im.astype(F32)
    bb_re = coef_re[..., None] * br - coef_im[..., None] * bi
    bb_im = coef_re[..., None] * bi + coef_im[..., None] * br

    def power(k):
        return jnp.exp(lr * dt * k) * jnp.cos(li * dt * k), jnp.exp(lr * dt * k) * jnp.sin(li * dt * k)

    kk = jnp.arange(t + 1, dtype=F32)[:, None, None]
    pw_re, pw_im = power(kk)
    wb_re = pw_re[:t, ..., None] * bb_re - pw_im[:t, ..., None] * bb_im
    wb_im = pw_re[:t, ..., None] * bb_im + pw_im[:t, ..., None] * bb_re
    wk = jnp.concatenate([wb_re, wb_im], axis=2).transpose(1, 0, 3, 2)
    ms = wk[:, ::-1].reshape(g, t * h, 2 * p)
    cre, cim = c_re.astype(F32), c_im.astype(F32)
    cc = jnp.concatenate([cre.transpose(0, 2, 1), -cim.transpose(0, 2, 1)], axis=1)
    taps = _s5_kernel_taps(wk.reshape(g, t * h, 2 * p), cc)
    kt = taps.reshape(g, t, h, h).transpose(0, 2, 1, 3).reshape(g, h, t * h)
    kpad = jnp.concatenate([jnp.zeros_like(kt), kt], axis=2)
    mi = jnp.stack([kpad[:, :, (t - s) * h:(2 * t - s) * h] for s in range(t)], axis=1)
    mi = mi.reshape(g, t * h, t * h)
    cre_t, cim_t = cre.transpose(0, 2, 1)[:, :, None, :], cim.transpose(0, 2, 1)[:, :, None, :]
    are = pw_re[1:].transpose(1, 2, 0)[..., None]
    aim = pw_im[1:].transpose(1, 2, 0)[..., None]
    mo_re = cre_t * are - cim_t * aim
    mo_im = -(cre_t * aim + cim_t * are)
    mo = jnp.concatenate([mo_re, mo_im], axis=1).reshape(g, 2 * p, t * h)
    nlev = max(1, (chunks - 1).bit_length())
    lv = (float(t) * (2.0 ** jnp.arange(nlev, dtype=F32)))[None, :, None]
    ar, ai = power_levels(lr, li, dt, lv)
    a1 = jnp.concatenate([ar, ar], axis=2)
    a2 = jnp.concatenate([-ai, ai], axis=2)
    dsk = jnp.tile(d_skip.astype(F32).reshape(g, 1, h), (1, 1, t))
    return mi.astype(BF16), ms.astype(BF16), mo.astype(BF16), a1, a2, dsk


def power_levels(lr, li, dt, lv):
    e = (lr * dt)[:, None, :] * lv
    w = (li * dt)[:, None, :] * lv
    return jnp.exp(e) * jnp.cos(w), jnp.exp(e) * jnp.sin(w)


def _mix_out_kernel(y_ref, att_ref, x_ref, wglu_ref, bglu_ref, gs_ref, ga_ref, woa_ref, wos_ref, o_ref):
    y = jax.nn.gelu(y_ref[...])
    z = _dot(y.astype(BF16), wglu_ref[...]) + bglu_ref[...]
    ssm = y * jax.nn.sigmoid(z)
    na = _rms(att_ref[...], ga_ref[...]).astype(BF16)
    ns = _rms(ssm, gs_ref[...]).astype(BF16)
    o_ref[...] = x_ref[...] + _dot(na, woa_ref[...]) + _dot(ns, wos_ref[...])


def _mix_out(ypre, att, x, w_glu, b_glu, g_ssm, g_att, wo_a, wo_s, tm):
    n, d = x.shape
    sw, aw = ypre.shape[1], att.shape[1]
    tm = min(tm, n)
    row = lambda i: (i, 0)
    fix = lambda i: (0, 0)
    once = dict(pipeline_mode=pl.Buffered(1))
    return pl.pallas_call(
        _mix_out_kernel,
        grid=(n // tm,),
        in_specs=[pl.BlockSpec((tm, sw), row), pl.BlockSpec((tm, aw), row), pl.BlockSpec((tm, d), row),
                  pl.BlockSpec((sw, sw), fix, **once), pl.BlockSpec((1, sw), fix),
                  pl.BlockSpec((1, sw), fix), pl.BlockSpec((1, aw), fix),
                  pl.BlockSpec((aw, d), fix, **once), pl.BlockSpec((sw, d), fix, **once)],
        out_specs=pl.BlockSpec((tm, d), row),
        out_shape=jax.ShapeDtypeStruct((n, d), F32),
        compiler_params=_cparams(("parallel",), VMEM_LIMIT),
        name="mix_out",
    )(ypre, att, x, w_glu, b_glu.reshape(1, sw), g_ssm.reshape(1, sw), g_att.reshape(1, aw), wo_a, wo_s)


def _cross_kernel(h_ref, g_ref, wq_ref, kv_ref, wo_ref, o_ref, *, heads):
    h = h_ref[...]
    d = h.shape[1]
    dh = d // heads
    q = _dot(_rms(h, g_ref[...]).astype(BF16), wq_ref[...]).astype(BF16)
    out = h
    for hd in range(heads):
        lo = hd * dh
        s = _dot_t(q[:, lo:lo + dh], kv_ref[:, lo:lo + dh]) * dh ** -0.5
        e = jnp.exp(s - jnp.max(s, axis=1, keepdims=True))
        p = e / jnp.sum(e, axis=1, keepdims=True)
        oh = _dot(p.astype(BF16), kv_ref[:, d + lo:d + lo + dh])
        out = out + _dot(oh.astype(BF16), wo_ref[lo:lo + dh, :])
    o_ref[...] = out


def _cross_attention(h, g, wq, kv, wo, *, batch, mem_len, tm):
    n, d = h.shape
    tm = min(tm, n // batch)
    tiles = n // batch // tm
    fix = lambda i: (0, 0)
    once = dict(pipeline_mode=pl.Buffered(1))
    return pl.pallas_call(
        functools.partial(_cross_kernel, heads=CROSS_HEADS),
        grid=(n // tm,),
        in_specs=[pl.BlockSpec((tm, d), lambda i: (i, 0)),
                  pl.BlockSpec((1, d), fix),
                  pl.BlockSpec((d, d), fix, **once),
                  pl.BlockSpec((mem_len, 2 * d), lambda i: (i // tiles, 0)),
                  pl.BlockSpec((d, d), fix, **once)],
        out_specs=pl.BlockSpec((tm, d), lambda i: (i, 0)),
        out_shape=jax.ShapeDtypeStruct((n, d), F32),
        compiler_params=_cparams(("parallel",), VMEM_LIMIT),
        name="cross_attention",
    )(h, g.reshape(1, d), wq, kv, wo)


def _router_kernel(h_ref, g_ref, wr_ref, br_ref, xm_ref, idx_ref, gate_ref, rank_ref, cnt_ref, carry_sc):
    @pl.when(pl.program_id(0) == 0)
    def _():
        carry_sc[...] = jnp.zeros(carry_sc.shape, F32)

    hm = _rms(h_ref[...], g_ref[...])
    xm_ref[...] = hm
    logits = _dot(hm, wr_ref[...], HIGHEST) + br_ref[...]
    tm = logits.shape[0]
    lane = lax.broadcasted_iota(I32, logits.shape, 1).astype(F32)
    work = logits
    vals, idxs = [], []
    for _ in range(TOP_K):
        m = jnp.max(work, axis=1, keepdims=True)
        am = jnp.min(jnp.where(work == m, lane, float(LANES)), axis=1, keepdims=True)
        vals.append(m)
        idxs.append(am)
        work = jnp.where(lane == am, NEG_BIG, work)
    es = [jnp.exp(v - vals[0]) for v in vals]
    den = es[0] + es[1] + es[2] + es[3]
    sel = jnp.zeros(logits.shape, F32)
    for am in idxs:
        sel = sel + (lane == am).astype(F32)
    r = lax.broadcasted_iota(I32, (tm, tm), 0)
    c = lax.broadcasted_iota(I32, (tm, tm), 1)
    before = (c < r).astype(BF16)
    rank_all = carry_sc[0:1, :] + _dot(before, sel.astype(BF16))
    idx_o = jnp.zeros(logits.shape, F32)
    gate_o = jnp.zeros(logits.shape, F32)
    rank_o = jnp.zeros(logits.shape, F32)
    for j in range(TOP_K):
        slot = lane == float(j)
        rk = jnp.sum(jnp.where(lane == idxs[j], rank_all, 0.0), axis=1, keepdims=True)
        idx_o = jnp.where(slot, idxs[j], idx_o)
        gate_o = jnp.where(slot, es[j] / den, gate_o)
        rank_o = jnp.where(slot, rk, rank_o)
    idx_ref[...] = idx_o.astype(I32)
    gate_ref[...] = gate_o
    rank_ref[...] = rank_o.astype(I32)
    carry_sc[...] = carry_sc[...] + jnp.sum(sel, axis=0, keepdims=True)
    cnt_ref[...] = carry_sc[...].astype(I32)


def _router(h, g, wr, br, tm):
    n, d = h.shape
    tm = min(tm, n)
    row = lambda i: (i, 0)
    fix = lambda i: (0, 0)
    return pl.pallas_call(
        _router_kernel,
        grid=(n // tm,),
        in_specs=[pl.BlockSpec((tm, d), row), pl.BlockSpec((1, d), fix),
                  pl.BlockSpec((d, LANES), fix), pl.BlockSpec((1, LANES), fix)],
        out_specs=[pl.BlockSpec((tm, d), row), pl.BlockSpec((tm, LANES), row),
                   pl.BlockSpec((tm, LANES), row), pl.BlockSpec((tm, LANES), row),
                   pl.BlockSpec((8, LANES), fix)],
        out_shape=[jax.ShapeDtypeStruct((n, d), F32), jax.ShapeDtypeStruct((n, LANES), I32),
                   jax.ShapeDtypeStruct((n, LANES), F32), jax.ShapeDtypeStruct((n, LANES), I32),
                   jax.ShapeDtypeStruct((8, LANES), I32)],
        scratch_shapes=[pltpu.VMEM((8, LANES), F32)],
        compiler_params=_cparams(("arbitrary",), VMEM_LIMIT),
        name="router",
    )(h, g.reshape(1, d), wr, br)


def _row_copy(src_hbm, row, dst, slot, sem):
    return pltpu.make_async_copy(src_hbm.at[pl.ds(row, 1), :], dst.at[pl.ds(slot, 1), :], sem)


def _slab_copy(src_hbm, dst, slot, rows, sem):
    return pltpu.make_async_copy(src_hbm.at[pl.ds(0, rows), :], dst.at[pl.ds(slot, rows), :], sem)


def _expert_kernel(sbe_ref, sbn_ref, nused_ref, tok_ref, xm_hbm, wg_ref, wl_ref, wd_ref, bg_ref, bl_ref,
                   bd_ref, o_ref, xg_ref, xb_ref, wgb_ref, wlb_ref, wdb_ref, sem):
    i = pl.program_id(0)
    j = pl.program_id(1)
    nrows = sbn_ref[i]
    nsub = (nrows + MOE_SUB - 1) // MOE_SUB

    @pl.when(jnp.logical_and(j == 0, nrows > 0))
    def _():
        def issue(r8, c):
            for u in range(ISSUE_UNROLL):
                r = r8 * ISSUE_UNROLL + u
                _row_copy(xm_hbm, tok_ref[0, 0, r], xg_ref, r, sem).start()
            return c

        lax.fori_loop(0, nsub * (MOE_SUB // ISSUE_UNROLL), issue, 0)
        for s in range(MOE_SUBS):
            rows = pl.ds(s * MOE_SUB, MOE_SUB)

            @pl.when(s < nsub)
            def _():
                o_ref[rows, :] = jnp.broadcast_to(bd_ref[...], (MOE_SUB, o_ref.shape[1]))

            @pl.when(s >= nsub)
            def _():
                o_ref[rows, :] = jnp.zeros((MOE_SUB, o_ref.shape[1]), F32)

        for s in range(MOE_SUBS):
            @pl.when(s < nsub)
            def _():
                _slab_copy(xm_hbm, xg_ref, s * MOE_SUB, MOE_SUB, sem).wait()

        for s in range(MOE_SUBS):
            rows = pl.ds(s * MOE_SUB, MOE_SUB)

            @pl.when(s < nsub)
            def _():
                xb_ref[rows, :] = xg_ref[rows, :].astype(BF16)

    def ffn(rows):
        x = xb_ref[rows, :]
        g = jnp.minimum(_dot(x, wgb_ref[...]) + bg_ref[...], SWIGLU_LIMIT)
        lin = jnp.clip(_dot(x, wlb_ref[...]) + bl_ref[...], -SWIGLU_LIMIT, SWIGLU_LIMIT)
        act = (g * jax.nn.sigmoid(SWIGLU_ALPHA * g) * (lin + 1.0)).astype(BF16)
        d = o_ref.shape[1]
        for c in range(0, d, MOE_NCHUNK):
            cols = pl.ds(c, min(MOE_NCHUNK, d - c))
            o_ref[rows, cols] = o_ref[rows, cols] + _dot(act, wdb_ref[:, cols])

    @pl.when(nrows > 0)
    def _():
        wgb_ref[...] = wg_ref[...].astype(BF16)
        wlb_ref[...] = wl_ref[...].astype(BF16)
        wdb_ref[...] = wd_ref[...].astype(BF16)

    @pl.when(nsub == MOE_SUBS)
    def _():
        ffn(pl.ds(0, MOE_SUBS * MOE_SUB))

    for s in range(MOE_SUBS - 1):
        @pl.when(jnp.logical_and(s < nsub, nsub < MOE_SUBS))
        def _():
            ffn(pl.ds(s * MOE_SUB, MOE_SUB))


def _experts(xm, row_tok, sb_e, sb_n, n_used, w_gu, b_gu, w_dn, b_dn):
    n_sb = sb_e.shape[0]
    sb_rows = MOE_SUBS * MOE_SUB
    e, d, f2 = w_gu.shape
    f = f2 // 2
    tf = min(MOE_FTILE, f)
    nf = f // tf

    def jj(i, j, nused):
        return jnp.where(i < nused[0], j, nf - 1)

    grid_spec = pltpu.PrefetchScalarGridSpec(
        num_scalar_prefetch=3,
        grid=(n_sb, nf),
        in_specs=[
            pl.BlockSpec((1, 1, sb_rows), lambda i, j, sbe, sbn, nu: (i, 0, 0), memory_space=pltpu.SMEM),
            pl.BlockSpec(memory_space=pl.ANY),
            pl.BlockSpec((None, d, tf), lambda i, j, sbe, sbn, nu: (sbe[i], 0, jj(i, j, nu))),
            pl.BlockSpec((None, d, tf), lambda i, j, sbe, sbn, nu: (sbe[i], 0, nf + jj(i, j, nu))),
            pl.BlockSpec((None, tf, d), lambda i, j, sbe, sbn, nu: (sbe[i], jj(i, j, nu), 0)),
            pl.BlockSpec((None, 1, tf), lambda i, j, sbe, sbn, nu: (sbe[i], 0, jj(i, j, nu))),
            pl.BlockSpec((None, 1, tf), lambda i, j, sbe, sbn, nu: (sbe[i], 0, nf + jj(i, j, nu))),
            pl.BlockSpec((None, 1, d), lambda i, j, sbe, sbn, nu: (sbe[i], 0, 0)),
        ],
        out_specs=pl.BlockSpec((sb_rows, d), lambda i, j, sbe, sbn, nu: (jnp.where(i < nu[0], i, n_sb), 0)),
        scratch_shapes=[pltpu.VMEM((sb_rows, d), F32), pltpu.VMEM((sb_rows, d), BF16),
                        pltpu.VMEM((d, tf), BF16), pltpu.VMEM((d, tf), BF16), pltpu.VMEM((tf, d), BF16),
                        pltpu.SemaphoreType.DMA],
    )
    return pl.pallas_call(
        _expert_kernel,
        grid_spec=grid_spec,
        out_shape=jax.ShapeDtypeStruct(((n_sb + 1) * sb_rows, d), F32),
        compiler_params=_cparams(("arbitrary", "arbitrary"), VMEM_LIMIT),
        name="experts",
    )(sb_e, sb_n, n_used, row_tok.reshape(n_sb, 1, sb_rows), xm, w_gu, w_gu, w_dn,
      b_gu.reshape(e, 1, f2), b_gu.reshape(e, 1, f2), b_dn.reshape(e, 1, d))


def _combine_kernel(dest_ref, gate_ref, h_ref, g_ref, y_hbm, o_ref, buf_ref, sem):
    tm = h_ref.shape[0]

    tok_unroll = ISSUE_UNROLL // TOP_K

    def issue(t2, c):
        for u in range(tok_unroll):
            t = t2 * tok_unroll + u
            for k in range(TOP_K):
                _row_copy(y_hbm, dest_ref[0, 0, t * TOP_K + k], buf_ref.at[k], t, sem).start()
        return c

    lax.fori_loop(0, tm // tok_unroll, issue, 0)
    acc = h_ref[...]
    gates = gate_ref[...]
    for k in range(TOP_K):
        _slab_copy(y_hbm, buf_ref.at[k], 0, tm, sem).wait()
    for k in range(TOP_K):
        acc = acc + gates[:, k:k + 1] * buf_ref[k]
    o_ref[...] = _rms(acc, g_ref[...])


def _combine(dest, gates, h, g, yrows, tm):
    n, d = h.shape
    tm = min(tm, n)
    return pl.pallas_call(
        _combine_kernel,
        grid=(n // tm,),
        in_specs=[pl.BlockSpec((1, 1, tm * TOP_K), lambda i: (i, 0, 0), memory_space=pltpu.SMEM),
                  pl.BlockSpec((tm, LANES), lambda i: (i, 0)),
                  pl.BlockSpec((tm, d), lambda i: (i, 0)),
                  pl.BlockSpec((1, d), lambda i: (0, 0)),
                  pl.BlockSpec(memory_space=pl.ANY)],
        out_specs=pl.BlockSpec((tm, d), lambda i: (i, 0)),
        out_shape=jax.ShapeDtypeStruct((n, d), F32),
        scratch_shapes=[pltpu.VMEM((TOP_K, tm, d), F32), pltpu.SemaphoreType.DMA],
        compiler_params=_cparams(("arbitrary",), VMEM_LIMIT),
        name="combine",
    )(dest.reshape(n // tm, 1, tm * TOP_K), gates, h, g.reshape(1, d), yrows)


def _layer(h, mem2d, p, *, batch, length):
    n, d = h.shape
    g_cnt, p_state = p["lam_re"].shape
    sw = g_cnt * SSM_GROUP
    aw = d - sw
    heads = aw // ATTN_HEAD_DIM
    assert p["w_in"].shape[1] == sw + 3 * aw + heads and sw % LANES == 0 and 2 * p_state == LANES

    w_in = p["w_in"]
    cols = sw + 3 * aw
    q_scale = ATTN_HEAD_DIM ** -0.5 * LOG2E
    cs = jnp.concatenate([jnp.ones((sw,), F32), jnp.full((aw,), q_scale, F32), jnp.ones((2 * aw,), F32)])
    proj = _norm_matmul(h, p["g_mix"], w_in[:, :cols].astype(BF16), cs, BF16, 512, 512)
    w_f = jnp.pad(w_in[:, cols:], ((0, 0), (0, LANES - heads)))
    flog = _norm_matmul(h, p["g_mix"], w_f, jnp.ones((LANES,), F32), F32, 512, LANES, HIGHEST)

    f_rows = flog[:, :heads].reshape(batch, length, heads).transpose(0, 2, 1).reshape(batch * heads, length)
    bias_rows = jnp.tile(p["b_f"].astype(F32), batch).reshape(batch * heads, 1)
    cum_rows = _fgate_cumsum(f_rows, bias_rows)
    cum_col = cum_rows.reshape(batch, heads, length).transpose(0, 2, 1).reshape(n, heads)

    blk = lambda c: c // ATTN_HEAD_DIM
    att = _fox_attention(proj, cum_col, cum_rows.reshape(batch * heads, 1, length), batch=batch, length=length,
                         heads=heads, q_col=blk(sw), k_col=blk(sw + aw), v_col=blk(sw + 2 * aw))

    t = min(S5_CHUNK, length)
    chunks = length // t
    mi, ms, mo, a1, a2, dsk = _s5_params(p["lam_re"], p["lam_im"], p["log_step"], p["b_re"], p["b_im"],
                                         p["c_re"], p["c_im"], p["d_skip"], t, chunks)
    ug = proj[:, :sw].reshape(batch, chunks, t, g_cnt, SSM_GROUP).transpose(3, 0, 1, 2, 4)
    ug = ug.reshape(g_cnt, batch * chunks, t * SSM_GROUP)
    yg = _s5_mixer(ug, mi, ms, mo, a1, a2, dsk, chunks=chunks)
    ypre = yg.reshape(g_cnt, batch, chunks, t, SSM_GROUP).transpose(1, 2, 3, 0, 4).reshape(n, sw)

    w_out = p["w_out"].astype(BF16)
    h = _mix_out(ypre, att, h, p["w_glu"].astype(BF16), p["b_glu"], p["g_ssm_out"], p["g_attn_out"],
                 w_out[:aw], w_out[aw:], 256)

    mem_len = mem2d.shape[0] // batch
    kv = _norm_matmul(mem2d, p["g_mem"], p["w_ckv"].astype(BF16), jnp.ones((2 * d,), F32), BF16, 512, 512)
    h = _cross_attention(h, p["g_cross"], p["w_cq"].astype(BF16), kv, p["w_co"].astype(BF16),
                         batch=batch, mem_len=mem_len, tm=256)

    dest, gates, yrows = _moe(h, p)
    return dest, gates, h, yrows


def _moe(h, p):
    n = h.shape[0]
    n_exp = p["w_router"].shape[1]
    wr = jnp.pad(p["w_router"].astype(F32), ((0, 0), (0, LANES - n_exp)))
    br = jnp.pad(p["b_router"].astype(F32), (0, LANES - n_exp), constant_values=NEG_BIG).reshape(1, LANES)
    xm, idx, gates, rank, cnt = _router(h, p["g_moe"], wr, br, 256)

    sb_rows = MOE_SUBS * MOE_SUB
    n_sb = (n * TOP_K) // sb_rows + n_exp
    counts = cnt[0, :n_exp]
    padded = (counts + sb_rows - 1) // sb_rows * sb_rows
    pend = jnp.cumsum(padded)
    pstart = pend - padded
    idx_k, rank_k = idx[:, :TOP_K], rank[:, :TOP_K]
    onehot = idx_k[..., None] == jnp.arange(n_exp, dtype=I32)
    dest = rank_k + jnp.sum(jnp.where(onehot, pstart, 0), axis=-1)
    tok = jnp.broadcast_to(jnp.arange(n, dtype=I32)[:, None], (n, TOP_K))
    row_tok = jnp.zeros((n_sb * sb_rows,), I32).at[dest.reshape(-1)].set(tok.reshape(-1), unique_indices=True)
    n_used = (pend[-1] // sb_rows).astype(I32)
    sb_start = jnp.arange(n_sb, dtype=I32) * sb_rows
    sb_e = jnp.clip(jnp.sum(sb_start[:, None] >= pend[None, :], axis=1), 0, n_exp - 1).astype(I32)
    used = jnp.arange(n_sb, dtype=I32) < n_used
    sb_n = jnp.where(used, jnp.clip(counts[sb_e] - (sb_start - pstart[sb_e]), 0, sb_rows), 0).astype(I32)
    sb_e = jnp.where(used, sb_e, sb_e[jnp.maximum(n_used - 1, 0)])

    yrows = _experts(xm, row_tok, sb_e, sb_n, n_used.reshape(1), p["w_gu"], p["b_gu"], p["w_dn"], p["b_dn"])
    return dest, gates, yrows


def kernel(x, mem, g_mix, w_in, b_f, lam_re, lam_im, log_step, b_re, b_im, c_re, c_im, d_skip, w_glu, b_glu, g_attn_out, g_ssm_out, w_out, g_cross, g_mem, w_cq, w_ckv, w_co, g_moe, w_router, b_router, w_gu, b_gu, w_dn, b_dn, g_final):
    batch, length, d = x.shape
    depth = g_mix.shape[0]
    assert depth == 1, "the combine kernel applies the final norm, so exactly one layer is supported"
    names = ("g_mix", "w_in", "b_f", "lam_re", "lam_im", "log_step", "b_re", "b_im", "c_re", "c_im", "d_skip",
             "w_glu", "b_glu", "g_attn_out", "g_ssm_out", "w_out", "g_cross", "g_mem", "w_cq", "w_ckv", "w_co",
             "g_moe", "w_router", "b_router", "w_gu", "b_gu", "w_dn", "b_dn")
    vals = (g_mix, w_in, b_f, lam_re, lam_im, log_step, b_re, b_im, c_re, c_im, d_skip, w_glu, b_glu,
            g_attn_out, g_ssm_out, w_out, g_cross, g_mem, w_cq, w_ckv, w_co, g_moe, w_router, b_router,
            w_gu, b_gu, w_dn, b_dn)
    params = {k: v[0] for k, v in zip(names, vals)}
    h = x.reshape(batch * length, d)
    mem2d = mem.reshape(-1, d)
    dest, gates, h, yrows = _layer(h, mem2d, params, batch=batch, length=length)
    out = _combine(dest, gates, h, g_final, yrows, 256)
    return out.reshape(batch, length, d)
```

```python
import functools

import jax
import jax.numpy as jnp
from jax import lax
from jax.experimental import pallas as pl
from jax.experimental.pallas import tpu as pltpu

F32 = jnp.float32
BF16 = jnp.bfloat16
I32 = jnp.int32

RMS_EPS = 1e-5
SSM_GROUP = 16
ATTN_HEAD_DIM = 128
CROSS_HEADS = 4
TOP_K = 4
SWIGLU_ALPHA = 1.702
SWIGLU_LIMIT = 7.0

LANES = 128
LOG2E = 1.4426950408889634
ATT_BLOCK = 1024
ATT_SQ = 128
ATT_SK = 256
S5_CHUNK = 32
MOE_SUB = 256
MOE_SUBS = 4
MOE_FTILE = 256
MOE_NCHUNK = 512
ISSUE_UNROLL = 8
NEG_BIG = -1e30
VMEM_LIMIT = 56 * 1024 * 1024

HIGHEST = lax.Precision.HIGHEST


def _cparams(sem, vmem=None):
    return pltpu.CompilerParams(dimension_semantics=sem, vmem_limit_bytes=vmem)


def _rms(x, g):
    return x * lax.rsqrt(jnp.mean(x * x, axis=-1, keepdims=True) + RMS_EPS) * g


def _dot(a, b, precision=None):
    return jnp.dot(a, b, preferred_element_type=F32, precision=precision)


def _dot_t(a, b):
    return lax.dot_general(a, b, (((1,), (1,)), ((), ())), preferred_element_type=F32)


def _norm_matmul_kernel(x_ref, g_ref, w_ref, cs_ref, o_ref, xn_ref, *, precision):
    @pl.when(pl.program_id(1) == 0)
    def _():
        xn_ref[...] = _rms(x_ref[...], g_ref[...]).astype(xn_ref.dtype)

    o_ref[...] = (_dot(xn_ref[...], w_ref[...], precision) * cs_ref[...]).astype(o_ref.dtype)


def _norm_matmul(x, g, w, col_scale, out_dtype, tm, tn, precision=None):
    n, k = x.shape
    m = w.shape[1]
    tm, tn = min(tm, n), min(tn, m)
    return pl.pallas_call(
        functools.partial(_norm_matmul_kernel, precision=precision),
        grid=(n // tm, m // tn),
        in_specs=[pl.BlockSpec((tm, k), lambda i, j: (i, 0)),
                  pl.BlockSpec((1, k), lambda i, j: (0, 0)),
                  pl.BlockSpec((k, tn), lambda i, j: (0, j)),
                  pl.BlockSpec((1, tn), lambda i, j: (0, j))],
        out_specs=pl.BlockSpec((tm, tn), lambda i, j: (i, j)),
        out_shape=jax.ShapeDtypeStruct((n, m), out_dtype),
        scratch_shapes=[pltpu.VMEM((tm, k), w.dtype)],
        compiler_params=_cparams(("parallel", "arbitrary"), VMEM_LIMIT),
        name="norm_matmul",
    )(x, g.reshape(1, k), w, col_scale.reshape(1, m))


def _fgate_cumsum_kernel(f_ref, b_ref, o_ref):
    z = f_ref[...] + b_ref[...]
    x = jnp.minimum(z, 0.0) - jnp.log(1.0 + jnp.exp(-jnp.abs(z)))
    lane = lax.broadcasted_iota(I32, x.shape, 1)
    d = 1
    while d < x.shape[1]:
        x = x + jnp.where(lane >= d, pltpu.roll(x, d, axis=1), 0.0)
        d *= 2
    o_ref[...] = x * LOG2E


def _fgate_cumsum(f_rows, bias_rows):
    return pl.pallas_call(
        _fgate_cumsum_kernel,
        out_shape=jax.ShapeDtypeStruct(f_rows.shape, F32),
        name="fgate_cumsum",
    )(f_rows, bias_rows)


def _fox_kernel(qi_ref, ki_ref, q_ref, k_ref, v_ref, cq_ref, ck_ref, o_ref, m_sc, l_sc, acc_sc, cq_sc,
                *, blk, sq, sk):
    h = pl.program_id(1)
    t = pl.program_id(2)
    qi = qi_ref[t]
    ki = ki_ref[t]
    rep = sk // LANES

    @pl.when(ki == 0)
    def _():
        m_sc[...] = jnp.full(m_sc.shape, -jnp.inf, F32)
        l_sc[...] = jnp.zeros(l_sc.shape, F32)
        acc_sc[...] = jnp.zeros(acc_sc.shape, F32)
        cq = cq_ref[...]
        lane = lax.broadcasted_iota(I32, cq.shape, 1)
        col = jnp.sum(jnp.where(lane == h, cq, 0.0), axis=1, keepdims=True)
        cq_sc[...] = jnp.broadcast_to(col, cq_sc.shape)

    def block(diag):
        for qs in range(blk // sq):
            r0 = qs * sq
            rows = pl.ds(r0, sq)
            q = q_ref[rows, :]
            cq = jnp.concatenate([cq_sc[rows, :]] * rep, axis=1)
            m, l, acc = m_sc[rows, :], l_sc[rows, :], acc_sc[rows, :]
            nkc = (r0 + sq - 1) // sk + 1 if diag else blk // sk
            for kc in range(nkc):
                c0 = kc * sk
                cols = pl.ds(c0, sk)
                s = _dot_t(q, k_ref[cols, :]) + cq - ck_ref[0, :, cols]
                if diag and c0 + sk - 1 > r0:
                    row = lax.broadcasted_iota(I32, s.shape, 0) + r0
                    col = lax.broadcasted_iota(I32, s.shape, 1) + c0
                    s = jnp.where(col <= row, s, -jnp.inf)
                m_new = jnp.maximum(m, jnp.max(s, axis=1, keepdims=True))
                alpha = jnp.exp2(m - m_new)
                p = jnp.exp2(s - jnp.concatenate([m_new] * rep, axis=1))
                psum = p[:, :LANES]
                for c in range(1, rep):
                    psum = psum + p[:, c * LANES:(c + 1) * LANES]
                l = alpha * l + psum
                acc = alpha * acc + _dot(p.astype(BF16), v_ref[cols, :])
                m = m_new
            if diag:
                o_ref[rows, :] = (acc / jnp.sum(l, axis=1, keepdims=True)).astype(o_ref.dtype)
            else:
                m_sc[rows, :], l_sc[rows, :], acc_sc[rows, :] = m, l, acc

    @pl.when(ki < qi)
    def _():
        block(False)

    @pl.when(ki == qi)
    def _():
        block(True)


def _fox_attention(proj, cum_col, cum_row, *, batch, length, heads, q_col, k_col, v_col):
    dh = ATTN_HEAD_DIM
    blk = min(ATT_BLOCK, length)
    sq, sk = min(ATT_SQ, blk), min(ATT_SK, blk)
    nq = length // blk
    n = batch * length
    pairs = [(a, b) for a in range(nq) for b in range(a + 1)]
    qi_tab = jnp.asarray([a for a, _ in pairs], I32)
    ki_tab = jnp.asarray([b for _, b in pairs], I32)
    kernel = functools.partial(_fox_kernel, blk=blk, sq=sq, sk=sk)
    grid_spec = pltpu.PrefetchScalarGridSpec(
        num_scalar_prefetch=2,
        grid=(batch, heads, len(pairs)),
        in_specs=[
            pl.BlockSpec((blk, dh), lambda b, h, t, qt, kt: (b * nq + qt[t], q_col + h)),
            pl.BlockSpec((blk, dh), lambda b, h, t, qt, kt: (b * nq + kt[t], k_col + h)),
            pl.BlockSpec((blk, dh), lambda b, h, t, qt, kt: (b * nq + kt[t], v_col + h)),
            pl.BlockSpec((blk, heads), lambda b, h, t, qt, kt: (b * nq + qt[t], 0)),
            pl.BlockSpec((1, 1, blk), lambda b, h, t, qt, kt: (b * heads + h, 0, kt[t])),
        ],
        out_specs=pl.BlockSpec((blk, dh), lambda b, h, t, qt, kt: (b * nq + qt[t], h)),
        scratch_shapes=[pltpu.VMEM((blk, LANES), F32), pltpu.VMEM((blk, LANES), F32),
                        pltpu.VMEM((blk, dh), F32), pltpu.VMEM((blk, LANES), F32)],
    )
    return pl.pallas_call(
        kernel,
        grid_spec=grid_spec,
        out_shape=jax.ShapeDtypeStruct((n, heads * dh), F32),
        compiler_params=_cparams(("parallel", "parallel", "arbitrary"), VMEM_LIMIT),
        name="fox_attention",
    )(qi_tab, ki_tab, proj, proj, proj, cum_col, cum_row)


def _s5_taps_kernel(wk_ref, cc_ref, o_ref):
    o_ref[0] = _dot(wk_ref[0], cc_ref[0], HIGHEST)


def _s5_kernel_taps(wk, cc):
    g, th, p2 = wk.shape
    h = cc.shape[2]
    return pl.pallas_call(
        _s5_taps_kernel,
        grid=(g,),
        in_specs=[pl.BlockSpec((1, th, p2), lambda i: (i, 0, 0)),
                  pl.BlockSpec((1, p2, h), lambda i: (i, 0, 0))],
        out_specs=pl.BlockSpec((1, th, h), lambda i: (i, 0, 0)),
        out_shape=jax.ShapeDtypeStruct((g, th, h), F32),
        compiler_params=_cparams(("parallel",)),
        name="s5_taps",
    )(wk, cc)


def _s5_mixer_kernel(u_ref, mi_ref, ms_ref, mo_ref, a1_ref, a2_ref, d_ref, y_ref, *, nlev, chunks, p):
    u = u_ref[0]
    s = _dot(u, ms_ref[0])
    row = lax.broadcasted_iota(I32, s.shape, 0) % chunks
    for j in range(nlev):
        d = 1 << j
        sh = jnp.where(row >= d, pltpu.roll(s, d, axis=0), 0.0)
        sw = pltpu.roll(sh, p, axis=1)
        s = s + a1_ref[0, j:j + 1, :] * sh + a2_ref[0, j:j + 1, :] * sw
    sp = jnp.where(row >= 1, pltpu.roll(s, 1, axis=0), 0.0)
    y = _dot(u, mi_ref[0]) + _dot(sp.astype(BF16), mo_ref[0]) + d_ref[0] * u.astype(F32)
    y_ref[0] = y


def _s5_mixer(ug, mi, ms, mo, a1, a2, dsk, *, chunks):
    g, r, th = ug.shape
    p2 = ms.shape[2]
    nlev = a1.shape[1]
    kernel = functools.partial(_s5_mixer_kernel, nlev=nlev, chunks=chunks, p=p2 // 2)
    return pl.pallas_call(
        kernel,
        grid=(g,),
        in_specs=[pl.BlockSpec((1, r, th), lambda i: (i, 0, 0)),
                  pl.BlockSpec((1, th, th), lambda i: (i, 0, 0)),
                  pl.BlockSpec((1, th, p2), lambda i: (i, 0, 0)),
                  pl.BlockSpec((1, p2, th), lambda i: (i, 0, 0)),
                  pl.BlockSpec((1, nlev, p2), lambda i: (i, 0, 0)),
                  pl.BlockSpec((1, nlev, p2), lambda i: (i, 0, 0)),
                  pl.BlockSpec((1, 1, th), lambda i: (i, 0, 0))],
        out_specs=pl.BlockSpec((1, r, th), lambda i: (i, 0, 0)),
        out_shape=jax.ShapeDtypeStruct((g, r, th), F32),
        compiler_params=_cparams(("parallel",), VMEM_LIMIT),
        name="s5_mixer",
    )(ug, mi, ms, mo, a1, a2, dsk)


def _s5_params(lam_re, lam_im, log_step, b_re, b_im, c_re, c_im, d_skip, t, chunks):
    g, p = lam_re.shape
    h = SSM_GROUP
    lr, li = lam_re.astype(F32), lam_im.astype(F32)
    dt = jnp.exp(log_step.astype(F32))[:, None]
    mag = jnp.exp(lr * dt)
    lb_re, lb_im = mag * jnp.cos(li * dt), mag * jnp.sin(li * dt)
    den = lr * lr + li * li
    nr, ni = lb_re - 1.0, lb_im
    coef_re = (nr * lr + ni * li) / den
    coef_im = (ni * lr - nr * li) / den
    br, bi = b_re.astype(F32), b_im.astype(F32)
    bb_re = coef_re[..., None] * br - coef_im[..., None] * bi
    bb_im = coef_re[..., None] * bi + coef_im[..., None] * br

    def power(k):
        return jnp.exp(lr * dt * k) * jnp.cos(li * dt * k), jnp.exp(lr * dt * k) * jnp.sin(li * dt * k)

    kk = jnp.arange(t + 1, dtype=F32)[:, None, None]
    pw_re, pw_im = power(kk)
    wb_re = pw_re[:t, ..., None] * bb_re - pw_im[:t, ..., None] * bb_im
    wb_im = pw_re[:t, ..., None] * bb_im + pw_im[:t, ..., None] * bb_re
    wk = jnp.concatenate([wb_re, wb_im], axis=2).transpose(1, 0, 3, 2)
    ms = wk[:, ::-1].reshape(g, t * h, 2 * p)
    cre, cim = c_re.astype(F32), c_im.astype(F32)
    cc = jnp.concatenate([cre.transpose(0, 2, 1), -cim.transpose(0, 2, 1)], axis=1)
    taps = _s5_kernel_taps(wk.reshape(g, t * h, 2 * p), cc)
    kt = taps.reshape(g, t, h, h).transpose(0, 2, 1, 3).reshape(g, h, t * h)
    kpad = jnp.concatenate([jnp.zeros_like(kt), kt], axis=2)
    mi = jnp.stack([kpad[:, :, (t - s) * h:(2 * t - s) * h] for s in range(t)], axis=1)
    mi = mi.reshape(g, t * h, t * h)
    cre_t, cim_t = cre.transpose(0, 2, 1)[:, :, None, :], cim.transpose(0, 2, 1)[:, :, None, :]
    are = pw_re[1:].transpose(1, 2, 0)[..., None]
    aim = pw_im[1:].transpose(1, 2, 0)[..., None]
    mo_re = cre_t * are - cim_t * aim
    mo_im = -(cre_t * aim + cim_t * are)
    mo = jnp.concatenate([mo_re, mo_im], axis=1).reshape(g, 2 * p, t * h)
    nlev = max(1, (chunks - 1).bit_length())
    lv = (float(t) * (2.0 ** jnp.arange(nlev, dtype=F32)))[None, :, None]
    ar, ai = power_levels(lr, li, dt, lv)
    a1 = jnp.concatenate([ar, ar], axis=2)
    a2 = jnp.concatenate([-ai, ai], axis=2)
    dsk = jnp.tile(d_skip.astype(F32).reshape(g, 1, h), (1, 1, t))
    return mi.astype(BF16), ms.astype(BF16), mo.astype(BF16), a1, a2, dsk


def power_levels(lr, li, dt, lv):
    e = (lr * dt)[:, None, :] * lv
    w = (li * dt)[:, None, :] * lv
    return jnp.exp(e) * jnp.cos(w), jnp.exp(e) * jnp.sin(w)


def _mix_out_kernel(y_ref, att_ref, x_ref, wglu_ref, bglu_ref, gs_ref, ga_ref, woa_ref, wos_ref, o_ref):
    y = jax.nn.gelu(y_ref[...])
    z = _dot(y.astype(BF16), wglu_ref[...]) + bglu_ref[...]
    ssm = y * jax.nn.sigmoid(z)
    na = _rms(att_ref[...], ga_ref[...]).astype(BF16)
    ns = _rms(ssm, gs_ref[...]).astype(BF16)
    o_ref[...] = x_ref[...] + _dot(na, woa_ref[...]) + _dot(ns, wos_ref[...])


def _mix_out(ypre, att, x, w_glu, b_glu, g_ssm, g_att, wo_a, wo_s, tm):
    n, d = x.shape
    sw, aw = ypre.shape[1], att.shape[1]
    tm = min(tm, n)
    row = lambda i: (i, 0)
    fix = lambda i: (0, 0)
    once = dict(pipeline_mode=pl.Buffered(1))
    return pl.pallas_call(
        _mix_out_kernel,
        grid=(n // tm,),
        in_specs=[pl.BlockSpec((tm, sw), row), pl.BlockSpec((tm, aw), row), pl.BlockSpec((tm, d), row),
                  pl.BlockSpec((sw, sw), fix, **once), pl.BlockSpec((1, sw), fix),
                  pl.BlockSpec((1, sw), fix), pl.BlockSpec((1, aw), fix),
                  pl.BlockSpec((aw, d), fix, **once), pl.BlockSpec((sw, d), fix, **once)],
        out_specs=pl.BlockSpec((tm, d), row),
        out_shape=jax.ShapeDtypeStruct((n, d), F32),
        compiler_params=_cparams(("parallel",), VMEM_LIMIT),
        name="mix_out",
    )(ypre, att, x, w_glu, b_glu.reshape(1, sw), g_ssm.reshape(1, sw), g_att.reshape(1, aw), wo_a, wo_s)


def _cross_kernel(h_ref, g_ref, wq_ref, kv_ref, wo_ref, o_ref, *, heads):
    h = h_ref[...]
    d = h.shape[1]
    dh = d // heads
    q = _dot(_rms(h, g_ref[...]).astype(BF16), wq_ref[...]).astype(BF16)
    out = h
    for hd in range(heads):
        lo = hd * dh
        s = _dot_t(q[:, lo:lo + dh], kv_ref[:, lo:lo + dh]) * dh ** -0.5
        e = jnp.exp(s - jnp.max(s, axis=1, keepdims=True))
        p = e / jnp.sum(e, axis=1, keepdims=True)
        oh = _dot(p.astype(BF16), kv_ref[:, d + lo:d + lo + dh])
        out = out + _dot(oh.astype(BF16), wo_ref[lo:lo + dh, :])
    o_ref[...] = out


def _cross_attention(h, g, wq, kv, wo, *, batch, mem_len, tm):
    n, d = h.shape
    tm = min(tm, n // batch)
    tiles = n // batch // tm
    fix = lambda i: (0, 0)
    once = dict(pipeline_mode=pl.Buffered(1))
    return pl.pallas_call(
        functools.partial(_cross_kernel, heads=CROSS_HEADS),
        grid=(n // tm,),
        in_specs=[pl.BlockSpec((tm, d), lambda i: (i, 0)),
                  pl.BlockSpec((1, d), fix),
                  pl.BlockSpec((d, d), fix, **once),
                  pl.BlockSpec((mem_len, 2 * d), lambda i: (i // tiles, 0)),
                  pl.BlockSpec((d, d), fix, **once)],
        out_specs=pl.BlockSpec((tm, d), lambda i: (i, 0)),
        out_shape=jax.ShapeDtypeStruct((n, d), F32),
        compiler_params=_cparams(("parallel",), VMEM_LIMIT),
        name="cross_attention",
    )(h, g.reshape(1, d), wq, kv, wo)


def _router_kernel(h_ref, g_ref, wr_ref, br_ref, xm_ref, idx_ref, gate_ref, rank_ref, cnt_ref, carry_sc):
    @pl.when(pl.program_id(0) == 0)
    def _():
        carry_sc[...] = jnp.zeros(carry_sc.shape, F32)

    hm = _rms(h_ref[...], g_ref[...])
    xm_ref[...] = hm
    logits = _dot(hm, wr_ref[...], HIGHEST) + br_ref[...]
    tm = logits.shape[0]
    lane = lax.broadcasted_iota(I32, logits.shape, 1).astype(F32)
    work = logits
    vals, idxs = [], []
    for _ in range(TOP_K):
        m = jnp.max(work, axis=1, keepdims=True)
        am = jnp.min(jnp.where(work == m, lane, float(LANES)), axis=1, keepdims=True)
        vals.append(m)
        idxs.append(am)
        work = jnp.where(lane == am, NEG_BIG, work)
    es = [jnp.exp(v - vals[0]) for v in vals]
    den = es[0] + es[1] + es[2] + es[3]
    sel = jnp.zeros(logits.shape, F32)
    for am in idxs:
        sel = sel + (lane == am).astype(F32)
    r = lax.broadcasted_iota(I32, (tm, tm), 0)
    c = lax.broadcasted_iota(I32, (tm, tm), 1)
    before = (c < r).astype(BF16)
    rank_all = carry_sc[0:1, :] + _dot(before, sel.astype(BF16))
    idx_o = jnp.zeros(logits.shape, F32)
    gate_o = jnp.zeros(logits.shape, F32)
    rank_o = jnp.zeros(logits.shape, F32)
    for j in range(TOP_K):
        slot = lane == float(j)
        rk = jnp.sum(jnp.where(lane == idxs[j], rank_all, 0.0), axis=1, keepdims=True)
        idx_o = jnp.where(slot, idxs[j], idx_o)
        gate_o = jnp.where(slot, es[j] / den, gate_o)
        rank_o = jnp.where(slot, rk, rank_o)
    idx_ref[...] = idx_o.astype(I32)
    gate_ref[...] = gate_o
    rank_ref[...] = rank_o.astype(I32)
    carry_sc[...] = carry_sc[...] + jnp.sum(sel, axis=0, keepdims=True)
    cnt_ref[...] = carry_sc[...].astype(I32)


def _router(h, g, wr, br, tm):
    n, d = h.shape
    tm = min(tm, n)
    row = lambda i: (i, 0)
    fix = lambda i: (0, 0)
    return pl.pallas_call(
        _router_kernel,
        grid=(n // tm,),
        in_specs=[pl.BlockSpec((tm, d), row), pl.BlockSpec((1, d), fix),
                  pl.BlockSpec((d, LANES), fix), pl.BlockSpec((1, LANES), fix)],
        out_specs=[pl.BlockSpec((tm, d), row), pl.BlockSpec((tm, LANES), row),
                   pl.BlockSpec((tm, LANES), row), pl.BlockSpec((tm, LANES), row),
                   pl.BlockSpec((8, LANES), fix)],
        out_shape=[jax.ShapeDtypeStruct((n, d), F32), jax.ShapeDtypeStruct((n, LANES), I32),
                   jax.ShapeDtypeStruct((n, LANES), F32), jax.ShapeDtypeStruct((n, LANES), I32),
                   jax.ShapeDtypeStruct((8, LANES), I32)],
        scratch_shapes=[pltpu.VMEM((8, LANES), F32)],
        compiler_params=_cparams(("arbitrary",), VMEM_LIMIT),
        name="router",
    )(h, g.reshape(1, d), wr, br)


def _row_copy(src_hbm, row, dst, slot, sem):
    return pltpu.make_async_copy(src_hbm.at[pl.ds(row, 1), :], dst.at[pl.ds(slot, 1), :], sem)


def _slab_copy(src_hbm, dst, slot, rows, sem):
    return pltpu.make_async_copy(src_hbm.at[pl.ds(0, rows), :], dst.at[pl.ds(slot, rows), :], sem)


def _expert_kernel(sbe_ref, sbn_ref, nused_ref, tok_ref, tokn_ref, xm_hbm, wg_ref, wl_ref, wd_ref, bg_ref,
                   bl_ref, bd_ref, o_ref, xg_ref, xb_ref, wgb_ref, wlb_ref, wdb_ref, sem, *, nf):
    i = pl.program_id(0)
    j = pl.program_id(1)
    n_i = pl.num_programs(0)
    sb_rows = xg_ref.shape[0]
    nrows = sbn_ref[i]
    nsub = (nrows + MOE_SUB - 1) // MOE_SUB

    def wait_rows():
        for s in range(MOE_SUBS):
            _slab_copy(xm_hbm, xg_ref, s * MOE_SUB, MOE_SUB, sem).wait()

    @pl.when(jnp.logical_and(i == 0, j == 0))
    def _():
        def issue(r8, c):
            for u in range(ISSUE_UNROLL):
                r = r8 * ISSUE_UNROLL + u
                _row_copy(xm_hbm, tok_ref[0, 0, r], xg_ref, r, sem).start()
            return c

        lax.fori_loop(0, sb_rows // ISSUE_UNROLL, issue, 0)

    @pl.when(j == 0)
    def _():
        for s in range(MOE_SUBS):
            rows = pl.ds(s * MOE_SUB, MOE_SUB)

            @pl.when(s < nsub)
            def _():
                o_ref[rows, :] = jnp.broadcast_to(bd_ref[...], (MOE_SUB, o_ref.shape[1]))

            @pl.when(s >= nsub)
            def _():
                o_ref[rows, :] = jnp.zeros((MOE_SUB, o_ref.shape[1]), F32)

        wait_rows()
        for s in range(MOE_SUBS):
            rows = pl.ds(s * MOE_SUB, MOE_SUB)

            @pl.when(s < nsub)
            def _():
                xb_ref[rows, :] = xg_ref[rows, :].astype(BF16)

    def prefetch_next():
        step_rows = sb_rows // nf
        for u in range(step_rows):
            r = j * step_rows + u
            _row_copy(xm_hbm, tokn_ref[0, 0, r], xg_ref, r, sem).start()

    def ffn(rows):
        x = xb_ref[rows, :]
        g = jnp.minimum(_dot(x, wgb_ref[...]) + bg_ref[...], SWIGLU_LIMIT)
        lin = jnp.clip(_dot(x, wlb_ref[...]) + bl_ref[...], -SWIGLU_LIMIT, SWIGLU_LIMIT)
        act = (g * jax.nn.sigmoid(SWIGLU_ALPHA * g) * (lin + 1.0)).astype(BF16)
        d = o_ref.shape[1]
        for c in range(0, d, MOE_NCHUNK):
            cols = pl.ds(c, min(MOE_NCHUNK, d - c))
            o_ref[rows, cols] = o_ref[rows, cols] + _dot(act, wdb_ref[:, cols])

    @pl.when(nrows > 0)
    def _():
        wgb_ref[...] = wg_ref[...].astype(BF16)
        wlb_ref[...] = wl_ref[...].astype(BF16)
        wdb_ref[...] = wd_ref[...].astype(BF16)

    @pl.when(nsub == MOE_SUBS)
    def _():
        prefetch_next()
        ffn(pl.ds(0, MOE_SUBS * MOE_SUB))

    @pl.when(nsub < MOE_SUBS)
    def _():
        prefetch_next()

    for s in range(MOE_SUBS - 1):
        @pl.when(jnp.logical_and(s < nsub, nsub < MOE_SUBS))
        def _():
            ffn(pl.ds(s * MOE_SUB, MOE_SUB))

    @pl.when(jnp.logical_and(i == n_i - 1, j == nf - 1))
    def _():
        wait_rows()


def _experts(xm, row_tok, sb_e, sb_n, n_used, w_gu, b_gu, w_dn, b_dn):
    n_sb = sb_e.shape[0]
    sb_rows = MOE_SUBS * MOE_SUB
    e, d, f2 = w_gu.shape
    f = f2 // 2
    tf = min(MOE_FTILE, f)
    nf = f // tf

    def jj(i, j, nused):
        return jnp.where(i < nused[0], j, nf - 1)

    grid_spec = pltpu.PrefetchScalarGridSpec(
        num_scalar_prefetch=3,
        grid=(n_sb, nf),
        in_specs=[
            pl.BlockSpec((1, 1, sb_rows), lambda i, j, sbe, sbn, nu: (i, 0, 0), memory_space=pltpu.SMEM),
            pl.BlockSpec((1, 1, sb_rows), lambda i, j, sbe, sbn, nu: (jnp.minimum(i + 1, n_sb - 1), 0, 0),
                         memory_space=pltpu.SMEM),
            pl.BlockSpec(memory_space=pl.ANY),
            pl.BlockSpec((None, d, tf), lambda i, j, sbe, sbn, nu: (sbe[i], 0, jj(i, j, nu))),
            pl.BlockSpec((None, d, tf), lambda i, j, sbe, sbn, nu: (sbe[i], 0, nf + jj(i, j, nu))),
            pl.BlockSpec((None, tf, d), lambda i, j, sbe, sbn, nu: (sbe[i], jj(i, j, nu), 0)),
            pl.BlockSpec((None, 1, tf), lambda i, j, sbe, sbn, nu: (sbe[i], 0, jj(i, j, nu))),
            pl.BlockSpec((None, 1, tf), lambda i, j, sbe, sbn, nu: (sbe[i], 0, nf + jj(i, j, nu))),
            pl.BlockSpec((None, 1, d), lambda i, j, sbe, sbn, nu: (sbe[i], 0, 0)),
        ],
        out_specs=pl.BlockSpec((sb_rows, d), lambda i, j, sbe, sbn, nu: (jnp.where(i < nu[0], i, n_sb), 0)),
        scratch_shapes=[pltpu.VMEM((sb_rows, d), F32), pltpu.VMEM((sb_rows, d), BF16),
                        pltpu.VMEM((d, tf), BF16), pltpu.VMEM((d, tf), BF16), pltpu.VMEM((tf, d), BF16),
                        pltpu.SemaphoreType.DMA],
    )
    return pl.pallas_call(
        functools.partial(_expert_kernel, nf=nf),
        grid_spec=grid_spec,
        out_shape=jax.ShapeDtypeStruct(((n_sb + 1) * sb_rows, d), F32),
        compiler_params=_cparams(("arbitrary", "arbitrary"), VMEM_LIMIT),
        name="experts",
    )(sb_e, sb_n, n_used, row_tok.reshape(n_sb, 1, sb_rows), row_tok.reshape(n_sb, 1, sb_rows), xm, w_gu, w_gu, w_dn,
      b_gu.reshape(e, 1, f2), b_gu.reshape(e, 1, f2), b_dn.reshape(e, 1, d))


def _combine_kernel(dest_ref, gate_ref, h_ref, g_ref, y_hbm, o_ref, buf_ref, sem):
    tm = h_ref.shape[0]

    tok_unroll = ISSUE_UNROLL // TOP_K

    def issue(t2, c):
        for u in range(tok_unroll):
            t = t2 * tok_unroll + u
            for k in range(TOP_K):
                _row_copy(y_hbm, dest_ref[0, 0, t * TOP_K + k], buf_ref.at[k], t, sem).start()
        return c

    lax.fori_loop(0, tm // tok_unroll, issue, 0)
    acc = h_ref[...]
    gates = gate_ref[...]
    for k in range(TOP_K):
        _slab_copy(y_hbm, buf_ref.at[k], 0, tm, sem).wait()
    for k in range(TOP_K):
        acc = acc + gates[:, k:k + 1] * buf_ref[k]
    o_ref[...] = _rms(acc, g_ref[...])


def _combine(dest, gates, h, g, yrows, tm):
    n, d = h.shape
    tm = min(tm, n)
    return pl.pallas_call(
        _combine_kernel,
        grid=(n // tm,),
        in_specs=[pl.BlockSpec((1, 1, tm * TOP_K), lambda i: (i, 0, 0), memory_space=pltpu.SMEM),
                  pl.BlockSpec((tm, LANES), lambda i: (i, 0)),
                  pl.BlockSpec((tm, d), lambda i: (i, 0)),
                  pl.BlockSpec((1, d), lambda i: (0, 0)),
                  pl.BlockSpec(memory_space=pl.ANY)],
        out_specs=pl.BlockSpec((tm, d), lambda i: (i, 0)),
        out_shape=jax.ShapeDtypeStruct((n, d), F32),
        scratch_shapes=[pltpu.VMEM((TOP_K, tm, d), F32), pltpu.SemaphoreType.DMA],
        compiler_params=_cparams(("arbitrary",), VMEM_LIMIT),
        name="combine",
    )(dest.reshape(n // tm, 1, tm * TOP_K), gates, h, g.reshape(1, d), yrows)


def _layer(h, mem2d, p, *, batch, length):
    n, d = h.shape
    g_cnt, p_state = p["lam_re"].shape
    sw = g_cnt * SSM_GROUP
    aw = d - sw
    heads = aw // ATTN_HEAD_DIM
    assert p["w_in"].shape[1] == sw + 3 * aw + heads and sw % LANES == 0 and 2 * p_state == LANES

    w_in = p["w_in"]
    cols = sw + 3 * aw
    q_scale = ATTN_HEAD_DIM ** -0.5 * LOG2E
    cs = jnp.concatenate([jnp.ones((sw,), F32), jnp.full((aw,), q_scale, F32), jnp.ones((2 * aw,), F32)])
    proj = _norm_matmul(h, p["g_mix"], w_in[:, :cols].astype(BF16), cs, BF16, 1024, 1024)
    w_f = jnp.pad(w_in[:, cols:], ((0, 0), (0, LANES - heads)))
    flog = _norm_matmul(h, p["g_mix"], w_f, jnp.ones((LANES,), F32), F32, 512, LANES, HIGHEST)

    f_rows = flog[:, :heads].reshape(batch, length, heads).transpose(0, 2, 1).reshape(batch * heads, length)
    bias_rows = jnp.tile(p["b_f"].astype(F32), batch).reshape(batch * heads, 1)
    cum_rows = _fgate_cumsum(f_rows, bias_rows)
    cum_col = cum_rows.reshape(batch, heads, length).transpose(0, 2, 1).reshape(n, heads)

    blk = lambda c: c // ATTN_HEAD_DIM
    att = _fox_attention(proj, cum_col, cum_rows.reshape(batch * heads, 1, length), batch=batch, length=length,
                         heads=heads, q_col=blk(sw), k_col=blk(sw + aw), v_col=blk(sw + 2 * aw))

    t = min(S5_CHUNK, length)
    chunks = length // t
    mi, ms, mo, a1, a2, dsk = _s5_params(p["lam_re"], p["lam_im"], p["log_step"], p["b_re"], p["b_im"],
                                         p["c_re"], p["c_im"], p["d_skip"], t, chunks)
    ug = proj[:, :sw].reshape(batch, chunks, t, g_cnt, SSM_GROUP).transpose(3, 0, 1, 2, 4)
    ug = ug.reshape(g_cnt, batch * chunks, t * SSM_GROUP)
    yg = _s5_mixer(ug, mi, ms, mo, a1, a2, dsk, chunks=chunks)
    ypre = yg.reshape(g_cnt, batch, chunks, t, SSM_GROUP).transpose(1, 2, 3, 0, 4).reshape(n, sw)

    w_out = p["w_out"].astype(BF16)
    h = _mix_out(ypre, att, h, p["w_glu"].astype(BF16), p["b_glu"], p["g_ssm_out"], p["g_attn_out"],
                 w_out[:aw], w_out[aw:], 256)

    mem_len = mem2d.shape[0] // batch
    kv = _norm_matmul(mem2d, p["g_mem"], p["w_ckv"].astype(BF16), jnp.ones((2 * d,), F32), BF16, 512, 512)
    h = _cross_attention(h, p["g_cross"], p["w_cq"].astype(BF16), kv, p["w_co"].astype(BF16),
                         batch=batch, mem_len=mem_len, tm=512)

    dest, gates, yrows = _moe(h, p)
    return dest, gates, h, yrows


def _moe(h, p):
    n = h.shape[0]
    n_exp = p["w_router"].shape[1]
    wr = jnp.pad(p["w_router"].astype(F32), ((0, 0), (0, LANES - n_exp)))
    br = jnp.pad(p["b_router"].astype(F32), (0, LANES - n_exp), constant_values=NEG_BIG).reshape(1, LANES)
    xm, idx, gates, rank, cnt = _router(h, p["g_moe"], wr, br, 256)

    sb_rows = MOE_SUBS * MOE_SUB
    n_sb = (n * TOP_K) // sb_rows + n_exp
    counts = cnt[0, :n_exp]
    padded = (counts + sb_rows - 1) // sb_rows * sb_rows
    pend = jnp.cumsum(padded)
    pstart = pend - padded
    idx_k, rank_k = idx[:, :TOP_K], rank[:, :TOP_K]
    onehot = idx_k[..., None] == jnp.arange(n_exp, dtype=I32)
    dest = rank_k + jnp.sum(jnp.where(onehot, pstart, 0), axis=-1)
    tok = jnp.broadcast_to(jnp.arange(n, dtype=I32)[:, None], (n, TOP_K))
    row_tok = jnp.zeros((n_sb * sb_rows,), I32).at[dest.reshape(-1)].set(tok.reshape(-1), unique_indices=True)
    n_used = (pend[-1] // sb_rows).astype(I32)
    sb_start = jnp.arange(n_sb, dtype=I32) * sb_rows
    sb_e = jnp.clip(jnp.sum(sb_start[:, None] >= pend[None, :], axis=1), 0, n_exp - 1).astype(I32)
    used = jnp.arange(n_sb, dtype=I32) < n_used
    sb_n = jnp.where(used, jnp.clip(counts[sb_e] - (sb_start - pstart[sb_e]), 0, sb_rows), 0).astype(I32)
    sb_e = jnp.where(used, sb_e, sb_e[jnp.maximum(n_used - 1, 0)])

    yrows = _experts(xm, row_tok, sb_e, sb_n, n_used.reshape(1), p["w_gu"], p["b_gu"], p["w_dn"], p["b_dn"])
    return dest, gates, yrows


def kernel(x, mem, g_mix, w_in, b_f, lam_re, lam_im, log_step, b_re, b_im, c_re, c_im, d_skip, w_glu, b_glu, g_attn_out, g_ssm_out, w_out, g_cross, g_mem, w_cq, w_ckv, w_co, g_moe, w_router, b_router, w_gu, b_gu, w_dn, b_dn, g_final):
    batch, length, d = x.shape
    depth = g_mix.shape[0]
    assert depth == 1, "the combine kernel applies the final norm, so exactly one layer is supported"
    names = ("g_mix", "w_in", "b_f", "lam_re", "lam_im", "log_step", "b_re", "b_im", "c_re", "c_im", "d_skip",
             "w_glu", "b_glu", "g_attn_out", "g_ssm_out", "w_out", "g_cross", "g_mem", "w_cq", "w_ckv", "w_co",
             "g_moe", "w_router", "b_router", "w_gu", "b_gu", "w_dn", "b_dn")
    vals = (g_mix, w_in, b_f, lam_re, lam_im, log_step, b_re, b_im, c_re, c_im, d_skip, w_glu, b_glu,
            g_attn_out, g_ssm_out, w_out, g_cross, g_mem, w_cq, w_ckv, w_co, g_moe, w_router, b_router,
            w_gu, b_gu, w_dn, b_dn)
    params = {k: v[0] for k, v in zip(names, vals)}
    h = x.reshape(batch * length, d)
    mem2d = mem.reshape(-1, d)
    dest, gates, h, yrows = _layer(h, mem2d, params, batch=batch, length=length)
    out = _combine(dest, gates, h, g_final, yrows, 256)
    return out.reshape(batch, length, d)
```

```python
import functools

import jax
import jax.numpy as jnp
from jax import lax
from jax.experimental import pallas as pl
from jax.experimental.pallas import tpu as pltpu

F32 = jnp.float32
BF16 = jnp.bfloat16
I32 = jnp.int32

RMS_EPS = 1e-5
SSM_GROUP = 16
ATTN_HEAD_DIM = 128
CROSS_HEADS = 4
TOP_K = 4
SWIGLU_ALPHA = 1.702
SWIGLU_LIMIT = 7.0

LANES = 128
LOG2E = 1.4426950408889634
ATT_BLOCK = 1024
ATT_SQ = 128
ATT_SK = 256
S5_CHUNK = 32
MOE_SUB = 256
MOE_SUBS = 4
MOE_FTILE = 256
MOE_NCHUNK = 512
MOE2_SB = 512
MOE2_VMEM_LIMIT = 60 * 1024 * 1024
ISSUE_UNROLL = 8
NEG_BIG = -1e30
VMEM_LIMIT = 56 * 1024 * 1024

HIGHEST = lax.Precision.HIGHEST


def _cparams(sem, vmem=None):
    return pltpu.CompilerParams(dimension_semantics=sem, vmem_limit_bytes=vmem)


def _rms(x, g):
    return x * lax.rsqrt(jnp.mean(x * x, axis=-1, keepdims=True) + RMS_EPS) * g


def _dot(a, b, precision=None):
    return jnp.dot(a, b, preferred_element_type=F32, precision=precision)


def _dot_t(a, b):
    return lax.dot_general(a, b, (((1,), (1,)), ((), ())), preferred_element_type=F32)


def _norm_matmul_kernel(x_ref, g_ref, w_ref, cs_ref, o_ref, xn_ref, *, precision):
    @pl.when(pl.program_id(1) == 0)
    def _():
        xn_ref[...] = _rms(x_ref[...], g_ref[...]).astype(xn_ref.dtype)

    o_ref[...] = (_dot(xn_ref[...], w_ref[...], precision) * cs_ref[...]).astype(o_ref.dtype)


def _norm_matmul(x, g, w, col_scale, out_dtype, tm, tn, precision=None):
    n, k = x.shape
    m = w.shape[1]
    tm, tn = min(tm, n), min(tn, m)
    return pl.pallas_call(
        functools.partial(_norm_matmul_kernel, precision=precision),
        grid=(n // tm, m // tn),
        in_specs=[pl.BlockSpec((tm, k), lambda i, j: (i, 0)),
                  pl.BlockSpec((1, k), lambda i, j: (0, 0)),
                  pl.BlockSpec((k, tn), lambda i, j: (0, j)),
                  pl.BlockSpec((1, tn), lambda i, j: (0, j))],
        out_specs=pl.BlockSpec((tm, tn), lambda i, j: (i, j)),
        out_shape=jax.ShapeDtypeStruct((n, m), out_dtype),
        scratch_shapes=[pltpu.VMEM((tm, k), w.dtype)],
        compiler_params=_cparams(("parallel", "arbitrary"), VMEM_LIMIT),
        name="norm_matmul",
    )(x, g.reshape(1, k), w, col_scale.reshape(1, m))


def _fgate_cumsum_kernel(f_ref, b_ref, o_ref):
    z = f_ref[...] + b_ref[...]
    x = jnp.minimum(z, 0.0) - jnp.log(1.0 + jnp.exp(-jnp.abs(z)))
    lane = lax.broadcasted_iota(I32, x.shape, 1)
    d = 1
    while d < x.shape[1]:
        x = x + jnp.where(lane >= d, pltpu.roll(x, d, axis=1), 0.0)
        d *= 2
    o_ref[...] = x * LOG2E


def _fgate_cumsum(f_rows, bias_rows):
    return pl.pallas_call(
        _fgate_cumsum_kernel,
        out_shape=jax.ShapeDtypeStruct(f_rows.shape, F32),
        name="fgate_cumsum",
    )(f_rows, bias_rows)


def _fox_kernel(qi_ref, ki_ref, q_ref, k_ref, v_ref, cq_ref, ck_ref, o_ref, m_sc, l_sc, acc_sc, cq_sc,
                *, blk, sq, sk):
    h = pl.program_id(1)
    t = pl.program_id(2)
    qi = qi_ref[t]
    ki = ki_ref[t]
    rep = sk // LANES

    @pl.when(ki == 0)
    def _():
        m_sc[...] = jnp.full(m_sc.shape, -jnp.inf, F32)
        l_sc[...] = jnp.zeros(l_sc.shape, F32)
        acc_sc[...] = jnp.zeros(acc_sc.shape, F32)
        cq = cq_ref[...]
        lane = lax.broadcasted_iota(I32, cq.shape, 1)
        col = jnp.sum(jnp.where(lane == h, cq, 0.0), axis=1, keepdims=True)
        cq_sc[...] = jnp.broadcast_to(col, cq_sc.shape)

    def block(diag):
        for qs in range(blk // sq):
            r0 = qs * sq
            rows = pl.ds(r0, sq)
            q = q_ref[rows, :]
            cq = jnp.concatenate([cq_sc[rows, :]] * rep, axis=1)
            m, l, acc = m_sc[rows, :], l_sc[rows, :], acc_sc[rows, :]
            nkc = (r0 + sq - 1) // sk + 1 if diag else blk // sk
            for kc in range(nkc):
                c0 = kc * sk
                cols = pl.ds(c0, sk)
                s = _dot_t(q, k_ref[cols, :]) + cq - ck_ref[0, :, cols]
                if diag and c0 + sk - 1 > r0:
                    row = lax.broadcasted_iota(I32, s.shape, 0) + r0
                    col = lax.broadcasted_iota(I32, s.shape, 1) + c0
                    s = jnp.where(col <= row, s, -jnp.inf)
                m_new = jnp.maximum(m, jnp.max(s, axis=1, keepdims=True))
                alpha = jnp.exp2(m - m_new)
                p = jnp.exp2(s - jnp.concatenate([m_new] * rep, axis=1))
                psum = p[:, :LANES]
                for c in range(1, rep):
                    psum = psum + p[:, c * LANES:(c + 1) * LANES]
                l = alpha * l + psum
                acc = alpha * acc + _dot(p.astype(BF16), v_ref[cols, :])
                m = m_new
            if diag:
                o_ref[rows, :] = (acc / jnp.sum(l, axis=1, keepdims=True)).astype(o_ref.dtype)
            else:
                m_sc[rows, :], l_sc[rows, :], acc_sc[rows, :] = m, l, acc

    @pl.when(ki < qi)
    def _():
        block(False)

    @pl.when(ki == qi)
    def _():
        block(True)


def _fox_attention(proj, cum_col, cum_row, *, batch, length, heads, q_col, k_col, v_col):
    dh = ATTN_HEAD_DIM
    blk = min(ATT_BLOCK, length)
    sq, sk = min(ATT_SQ, blk), min(ATT_SK, blk)
    nq = length // blk
    n = batch * length
    pairs = [(a, b) for a in range(nq) for b in range(a + 1)]
    qi_tab = jnp.asarray([a for a, _ in pairs], I32)
    ki_tab = jnp.asarray([b for _, b in pairs], I32)
    kernel = functools.partial(_fox_kernel, blk=blk, sq=sq, sk=sk)
    grid_spec = pltpu.PrefetchScalarGridSpec(
        num_scalar_prefetch=2,
        grid=(batch, heads, len(pairs)),
        in_specs=[
            pl.BlockSpec((blk, dh), lambda b, h, t, qt, kt: (b * nq + qt[t], q_col + h)),
            pl.BlockSpec((blk, dh), lambda b, h, t, qt, kt: (b * nq + kt[t], k_col + h)),
            pl.BlockSpec((blk, dh), lambda b, h, t, qt, kt: (b * nq + kt[t], v_col + h)),
            pl.BlockSpec((blk, heads), lambda b, h, t, qt, kt: (b * nq + qt[t], 0)),
            pl.BlockSpec((1, 1, blk), lambda b, h, t, qt, kt: (b * heads + h, 0, kt[t])),
        ],
        out_specs=pl.BlockSpec((blk, dh), lambda b, h, t, qt, kt: (b * nq + qt[t], h)),
        scratch_shapes=[pltpu.VMEM((blk, LANES), F32), pltpu.VMEM((blk, LANES), F32),
                        pltpu.VMEM((blk, dh), F32), pltpu.VMEM((blk, LANES), F32)],
    )
    return pl.pallas_call(
        kernel,
        grid_spec=grid_spec,
        out_shape=jax.ShapeDtypeStruct((n, heads * dh), F32),
        compiler_params=_cparams(("parallel", "parallel", "arbitrary"), VMEM_LIMIT),
        name="fox_attention",
    )(qi_tab, ki_tab, proj, proj, proj, cum_col, cum_row)


def _s5_taps_kernel(wk_ref, cc_ref, o_ref):
    o_ref[0] = _dot(wk_ref[0], cc_ref[0], HIGHEST)


def _s5_kernel_taps(wk, cc):
    g, th, p2 = wk.shape
    h = cc.shape[2]
    return pl.pallas_call(
        _s5_taps_kernel,
        grid=(g,),
        in_specs=[pl.BlockSpec((1, th, p2), lambda i: (i, 0, 0)),
                  pl.BlockSpec((1, p2, h), lambda i: (i, 0, 0))],
        out_specs=pl.BlockSpec((1, th, h), lambda i: (i, 0, 0)),
        out_shape=jax.ShapeDtypeStruct((g, th, h), F32),
        compiler_params=_cparams(("parallel",)),
        name="s5_taps",
    )(wk, cc)


def _s5_mixer_kernel(u_ref, mi_ref, ms_ref, mo_ref, a1_ref, a2_ref, d_ref, y_ref, *, nlev, chunks, p):
    u = u_ref[0]
    s = _dot(u, ms_ref[0])
    row = lax.broadcasted_iota(I32, s.shape, 0) % chunks
    for j in range(nlev):
        d = 1 << j
        sh = jnp.where(row >= d, pltpu.roll(s, d, axis=0), 0.0)
        sw = pltpu.roll(sh, p, axis=1)
        s = s + a1_ref[0, j:j + 1, :] * sh + a2_ref[0, j:j + 1, :] * sw
    sp = jnp.where(row >= 1, pltpu.roll(s, 1, axis=0), 0.0)
    y = _dot(u, mi_ref[0]) + _dot(sp.astype(BF16), mo_ref[0]) + d_ref[0] * u.astype(F32)
    y_ref[0] = y


def _s5_mixer(ug, mi, ms, mo, a1, a2, dsk, *, chunks):
    g, r, th = ug.shape
    p2 = ms.shape[2]
    nlev = a1.shape[1]
    kernel = functools.partial(_s5_mixer_kernel, nlev=nlev, chunks=chunks, p=p2 // 2)
    return pl.pallas_call(
        kernel,
        grid=(g,),
        in_specs=[pl.BlockSpec((1, r, th), lambda i: (i, 0, 0)),
                  pl.BlockSpec((1, th, th), lambda i: (i, 0, 0)),
                  pl.BlockSpec((1, th, p2), lambda i: (i, 0, 0)),
                  pl.BlockSpec((1, p2, th), lambda i: (i, 0, 0)),
                  pl.BlockSpec((1, nlev, p2), lambda i: (i, 0, 0)),
                  pl.BlockSpec((1, nlev, p2), lambda i: (i, 0, 0)),
                  pl.BlockSpec((1, 1, th), lambda i: (i, 0, 0))],
        out_specs=pl.BlockSpec((1, r, th), lambda i: (i, 0, 0)),
        out_shape=jax.ShapeDtypeStruct((g, r, th), F32),
        compiler_params=_cparams(("parallel",), VMEM_LIMIT),
        name="s5_mixer",
    )(ug, mi, ms, mo, a1, a2, dsk)


def _s5_params(lam_re, lam_im, log_step, b_re, b_im, c_re, c_im, d_skip, t, chunks):
    g, p = lam_re.shape
    h = SSM_GROUP
    lr, li = lam_re.astype(F32), lam_im.astype(F32)
    dt = jnp.exp(log_step.astype(F32))[:, None]
    mag = jnp.exp(lr * dt)
    lb_re, lb_im = mag * jnp.cos(li * dt), mag * jnp.sin(li * dt)
    den = lr * lr + li * li
    nr, ni = lb_re - 1.0, lb_im
    coef_re = (nr * lr + ni * li) / den
    coef_im = (ni * lr - nr * li) / den
    br, bi = b_re.astype(F32), b_im.astype(F32)
    bb_re = coef_re[..., None] * br - coef_im[..., None] * bi
    bb_im = coef_re[..., None] * bi + coef_im[..., None] * br

    def power(k):
        return jnp.exp(lr * dt * k) * jnp.cos(li * dt * k), jnp.exp(lr * dt * k) * jnp.sin(li * dt * k)

    kk = jnp.arange(t + 1, dtype=F32)[:, None, None]
    pw_re, pw_im = power(kk)
    wb_re = pw_re[:t, ..., None] * bb_re - pw_im[:t, ..., None] * bb_im
    wb_im = pw_re[:t, ..., None] * bb_im + pw_im[:t, ..., None] * bb_re
    wk = jnp.concatenate([wb_re, wb_im], axis=2).transpose(1, 0, 3, 2)
    ms = wk[:, ::-1].reshape(g, t * h, 2 * p)
    cre, cim = c_re.astype(F32), c_im.astype(F32)
    cc = jnp.concatenate([cre.transpose(0, 2, 1), -cim.transpose(0, 2, 1)], axis=1)
    taps = _s5_kernel_taps(wk.reshape(g, t * h, 2 * p), cc)
    kt = taps.reshape(g, t, h, h).transpose(0, 2, 1, 3).reshape(g, h, t * h)
    kpad = jnp.concatenate([jnp.zeros_like(kt), kt], axis=2)
    mi = jnp.stack([kpad[:, :, (t - s) * h:(2 * t - s) * h] for s in range(t)], axis=1)
    mi = mi.reshape(g, t * h, t * h)
    cre_t, cim_t = cre.transpose(0, 2, 1)[:, :, None, :], cim.transpose(0, 2, 1)[:, :, None, :]
    are = pw_re[1:].transpose(1, 2, 0)[..., None]
    aim = pw_im[1:].transpose(1, 2, 0)[..., None]
    mo_re = cre_t * are - cim_t * aim
    mo_im = -(cre_t * aim + cim_t * are)
    mo = jnp.concatenate([mo_re, mo_im], axis=1).reshape(g, 2 * p, t * h)
    nlev = max(1, (chunks - 1).bit_length())
    lv = (float(t) * (2.0 ** jnp.arange(nlev, dtype=F32)))[None, :, None]
    ar, ai = power_levels(lr, li, dt, lv)
    a1 = jnp.concatenate([ar, ar], axis=2)
    a2 = jnp.concatenate([-ai, ai], axis=2)
    dsk = jnp.tile(d_skip.astype(F32).reshape(g, 1, h), (1, 1, t))
    return mi.astype(BF16), ms.astype(BF16), mo.astype(BF16), a1, a2, dsk


def power_levels(lr, li, dt, lv):
    e = (lr * dt)[:, None, :] * lv
    w = (li * dt)[:, None, :] * lv
    return jnp.exp(e) * jnp.cos(w), jnp.exp(e) * jnp.sin(w)


def _mix_out_kernel(y_ref, att_ref, x_ref, wglu_ref, bglu_ref, gs_ref, ga_ref, woa_ref, wos_ref, o_ref):
    y = jax.nn.gelu(y_ref[...])
    z = _dot(y.astype(BF16), wglu_ref[...]) + bglu_ref[...]
    ssm = y * jax.nn.sigmoid(z)
    na = _rms(att_ref[...], ga_ref[...]).astype(BF16)
    ns = _rms(ssm, gs_ref[...]).astype(BF16)
    o_ref[...] = x_ref[...] + _dot(na, woa_ref[...]) + _dot(ns, wos_ref[...])


def _mix_out(ypre, att, x, w_glu, b_glu, g_ssm, g_att, wo_a, wo_s, tm):
    n, d = x.shape
    sw, aw = ypre.shape[1], att.shape[1]
    tm = min(tm, n)
    row = lambda i: (i, 0)
    fix = lambda i: (0, 0)
    once = dict(pipeline_mode=pl.Buffered(1))
    return pl.pallas_call(
        _mix_out_kernel,
        grid=(n // tm,),
        in_specs=[pl.BlockSpec((tm, sw), row), pl.BlockSpec((tm, aw), row), pl.BlockSpec((tm, d), row),
                  pl.BlockSpec((sw, sw), fix, **once), pl.BlockSpec((1, sw), fix),
                  pl.BlockSpec((1, sw), fix), pl.BlockSpec((1, aw), fix),
                  pl.BlockSpec((aw, d), fix, **once), pl.BlockSpec((sw, d), fix, **once)],
        out_specs=pl.BlockSpec((tm, d), row),
        out_shape=jax.ShapeDtypeStruct((n, d), F32),
        compiler_params=_cparams(("parallel",), VMEM_LIMIT),
        name="mix_out",
    )(ypre, att, x, w_glu, b_glu.reshape(1, sw), g_ssm.reshape(1, sw), g_att.reshape(1, aw), wo_a, wo_s)


def _cross_kernel(h_ref, g_ref, wq_ref, kv_ref, wo_ref, o_ref, *, heads):
    h = h_ref[...]
    d = h.shape[1]
    dh = d // heads
    q = _dot(_rms(h, g_ref[...]).astype(BF16), wq_ref[...]).astype(BF16)
    out = h
    for hd in range(heads):
        lo = hd * dh
        s = _dot_t(q[:, lo:lo + dh], kv_ref[:, lo:lo + dh]) * dh ** -0.5
        e = jnp.exp(s - jnp.max(s, axis=1, keepdims=True))
        p = e / jnp.sum(e, axis=1, keepdims=True)
        oh = _dot(p.astype(BF16), kv_ref[:, d + lo:d + lo + dh])
        out = out + _dot(oh.astype(BF16), wo_ref[lo:lo + dh, :])
    o_ref[...] = out


def _cross_attention(h, g, wq, kv, wo, *, batch, mem_len, tm):
    n, d = h.shape
    tm = min(tm, n // batch)
    tiles = n // batch // tm
    fix = lambda i: (0, 0)
    once = dict(pipeline_mode=pl.Buffered(1))
    return pl.pallas_call(
        functools.partial(_cross_kernel, heads=CROSS_HEADS),
        grid=(n // tm,),
        in_specs=[pl.BlockSpec((tm, d), lambda i: (i, 0)),
                  pl.BlockSpec((1, d), fix),
                  pl.BlockSpec((d, d), fix, **once),
                  pl.BlockSpec((mem_len, 2 * d), lambda i: (i // tiles, 0)),
                  pl.BlockSpec((d, d), fix, **once)],
        out_specs=pl.BlockSpec((tm, d), lambda i: (i, 0)),
        out_shape=jax.ShapeDtypeStruct((n, d), F32),
        compiler_params=_cparams(("parallel",), VMEM_LIMIT),
        name="cross_attention",
    )(h, g.reshape(1, d), wq, kv, wo)


def _router_kernel(h_ref, g_ref, wr_ref, br_ref, xm_ref, idx_ref, gate_ref, rank_ref, cnt_ref, carry_sc):
    @pl.when(pl.program_id(0) == 0)
    def _():
        carry_sc[...] = jnp.zeros(carry_sc.shape, F32)

    hm = _rms(h_ref[...], g_ref[...])
    xm_ref[...] = hm
    logits = _dot(hm, wr_ref[...], HIGHEST) + br_ref[...]
    tm = logits.shape[0]
    lane = lax.broadcasted_iota(I32, logits.shape, 1).astype(F32)
    work = logits
    vals, idxs = [], []
    for _ in range(TOP_K):
        m = jnp.max(work, axis=1, keepdims=True)
        am = jnp.min(jnp.where(work == m, lane, float(LANES)), axis=1, keepdims=True)
        vals.append(m)
        idxs.append(am)
        work = jnp.where(lane == am, NEG_BIG, work)
    es = [jnp.exp(v - vals[0]) for v in vals]
    den = es[0] + es[1] + es[2] + es[3]
    sel = jnp.zeros(logits.shape, F32)
    for am in idxs:
        sel = sel + (lane == am).astype(F32)
    r = lax.broadcasted_iota(I32, (tm, tm), 0)
    c = lax.broadcasted_iota(I32, (tm, tm), 1)
    before = (c < r).astype(BF16)
    rank_all = carry_sc[0:1, :] + _dot(before, sel.astype(BF16))
    idx_o = jnp.zeros(logits.shape, F32)
    gate_o = jnp.zeros(logits.shape, F32)
    rank_o = jnp.zeros(logits.shape, F32)
    for j in range(TOP_K):
        slot = lane == float(j)
        rk = jnp.sum(jnp.where(lane == idxs[j], rank_all, 0.0), axis=1, keepdims=True)
        idx_o = jnp.where(slot, idxs[j], idx_o)
        gate_o = jnp.where(slot, es[j] / den, gate_o)
        rank_o = jnp.where(slot, rk, rank_o)
    idx_ref[...] = idx_o.astype(I32)
    gate_ref[...] = gate_o
    rank_ref[...] = rank_o.astype(I32)
    carry_sc[...] = carry_sc[...] + jnp.sum(sel, axis=0, keepdims=True)
    cnt_ref[...] = carry_sc[...].astype(I32)


def _router(h, g, wr, br, tm):
    n, d = h.shape
    tm = min(tm, n)
    row = lambda i: (i, 0)
    fix = lambda i: (0, 0)
    return pl.pallas_call(
        _router_kernel,
        grid=(n // tm,),
        in_specs=[pl.BlockSpec((tm, d), row), pl.BlockSpec((1, d), fix),
                  pl.BlockSpec((d, LANES), fix), pl.BlockSpec((1, LANES), fix)],
        out_specs=[pl.BlockSpec((tm, d), row), pl.BlockSpec((tm, LANES), row),
                   pl.BlockSpec((tm, LANES), row), pl.BlockSpec((tm, LANES), row),
                   pl.BlockSpec((8, LANES), fix)],
        out_shape=[jax.ShapeDtypeStruct((n, d), F32), jax.ShapeDtypeStruct((n, LANES), I32),
                   jax.ShapeDtypeStruct((n, LANES), F32), jax.ShapeDtypeStruct((n, LANES), I32),
                   jax.ShapeDtypeStruct((8, LANES), I32)],
        scratch_shapes=[pltpu.VMEM((8, LANES), F32)],
        compiler_params=_cparams(("arbitrary",), VMEM_LIMIT),
        name="router",
    )(h, g.reshape(1, d), wr, br)


def _row_copy(src_hbm, row, dst, slot, sem):
    return pltpu.make_async_copy(src_hbm.at[pl.ds(row, 1), :], dst.at[pl.ds(slot, 1), :], sem)


def _slab_copy(src_hbm, dst, slot, rows, sem):
    return pltpu.make_async_copy(src_hbm.at[pl.ds(0, rows), :], dst.at[pl.ds(slot, rows), :], sem)


def _expert_kernel(sbe_ref, sbn_ref, nused_ref, tok_ref, tokn_ref, xm_hbm, wg_ref, wl_ref, wd_ref, bg_ref,
                   bl_ref, bd_ref, o_ref, xg_ref, xb_ref, wgb_ref, wlb_ref, wdb_ref, sem, *, nf):
    i = pl.program_id(0)
    j = pl.program_id(1)
    n_i = pl.num_programs(0)
    sb_rows = xg_ref.shape[0]
    nrows = sbn_ref[i]
    nsub = (nrows + MOE_SUB - 1) // MOE_SUB

    def wait_rows():
        for s in range(MOE_SUBS):
            _slab_copy(xm_hbm, xg_ref, s * MOE_SUB, MOE_SUB, sem).wait()

    @pl.when(jnp.logical_and(i == 0, j == 0))
    def _():
        def issue(r8, c):
            for u in range(ISSUE_UNROLL):
                r = r8 * ISSUE_UNROLL + u
                _row_copy(xm_hbm, tok_ref[0, 0, r], xg_ref, r, sem).start()
            return c

        lax.fori_loop(0, sb_rows // ISSUE_UNROLL, issue, 0)

    @pl.when(j == 0)
    def _():
        for s in range(MOE_SUBS):
            rows = pl.ds(s * MOE_SUB, MOE_SUB)

            @pl.when(s < nsub)
            def _():
                o_ref[rows, :] = jnp.broadcast_to(bd_ref[...], (MOE_SUB, o_ref.shape[1]))

            @pl.when(s >= nsub)
            def _():
                o_ref[rows, :] = jnp.zeros((MOE_SUB, o_ref.shape[1]), F32)

        wait_rows()
        for s in range(MOE_SUBS):
            rows = pl.ds(s * MOE_SUB, MOE_SUB)

            @pl.when(s < nsub)
            def _():
                xb_ref[rows, :] = xg_ref[rows, :].astype(BF16)

    def prefetch_next():
        step_rows = sb_rows // nf
        for u in range(step_rows):
            r = j * step_rows + u
            _row_copy(xm_hbm, tokn_ref[0, 0, r], xg_ref, r, sem).start()

    def ffn(rows):
        x = xb_ref[rows, :]
        g = jnp.minimum(_dot(x, wgb_ref[...]) + bg_ref[...], SWIGLU_LIMIT)
        lin = jnp.clip(_dot(x, wlb_ref[...]) + bl_ref[...], -SWIGLU_LIMIT, SWIGLU_LIMIT)
        act = (g * jax.nn.sigmoid(SWIGLU_ALPHA * g) * (lin + 1.0)).astype(BF16)
        d = o_ref.shape[1]
        for c in range(0, d, MOE_NCHUNK):
            cols = pl.ds(c, min(MOE_NCHUNK, d - c))
            o_ref[rows, cols] = o_ref[rows, cols] + _dot(act, wdb_ref[:, cols])

    @pl.when(nrows > 0)
    def _():
        wgb_ref[...] = wg_ref[...].astype(BF16)
        wlb_ref[...] = wl_ref[...].astype(BF16)
        wdb_ref[...] = wd_ref[...].astype(BF16)

    @pl.when(nsub == MOE_SUBS)
    def _():
        prefetch_next()
        ffn(pl.ds(0, MOE_SUBS * MOE_SUB))

    @pl.when(nsub < MOE_SUBS)
    def _():
        prefetch_next()

    for s in range(MOE_SUBS - 1):
        @pl.when(jnp.logical_and(s < nsub, nsub < MOE_SUBS))
        def _():
            ffn(pl.ds(s * MOE_SUB, MOE_SUB))

    @pl.when(jnp.logical_and(i == n_i - 1, j == nf - 1))
    def _():
        wait_rows()


def _experts(xm, row_tok, sb_e, sb_n, n_used, w_gu, b_gu, w_dn, b_dn):
    n_sb = sb_e.shape[0]
    sb_rows = MOE_SUBS * MOE_SUB
    e, d, f2 = w_gu.shape
    f = f2 // 2
    tf = min(MOE_FTILE, f)
    nf = f // tf

    def jj(i, j, nused):
        return jnp.where(i < nused[0], j, nf - 1)

    grid_spec = pltpu.PrefetchScalarGridSpec(
        num_scalar_prefetch=3,
        grid=(n_sb, nf),
        in_specs=[
            pl.BlockSpec((1, 1, sb_rows), lambda i, j, sbe, sbn, nu: (i, 0, 0), memory_space=pltpu.SMEM),
            pl.BlockSpec((1, 1, sb_rows), lambda i, j, sbe, sbn, nu: (jnp.minimum(i + 1, n_sb - 1), 0, 0),
                         memory_space=pltpu.SMEM),
            pl.BlockSpec(memory_space=pl.ANY),
            pl.BlockSpec((None, d, tf), lambda i, j, sbe, sbn, nu: (sbe[i], 0, jj(i, j, nu))),
            pl.BlockSpec((None, d, tf), lambda i, j, sbe, sbn, nu: (sbe[i], 0, nf + jj(i, j, nu))),
            pl.BlockSpec((None, tf, d), lambda i, j, sbe, sbn, nu: (sbe[i], jj(i, j, nu), 0)),
            pl.BlockSpec((None, 1, tf), lambda i, j, sbe, sbn, nu: (sbe[i], 0, jj(i, j, nu))),
            pl.BlockSpec((None, 1, tf), lambda i, j, sbe, sbn, nu: (sbe[i], 0, nf + jj(i, j, nu))),
            pl.BlockSpec((None, 1, d), lambda i, j, sbe, sbn, nu: (sbe[i], 0, 0)),
        ],
        out_specs=pl.BlockSpec((sb_rows, d), lambda i, j, sbe, sbn, nu: (jnp.where(i < nu[0], i, n_sb), 0)),
        scratch_shapes=[pltpu.VMEM((sb_rows, d), F32), pltpu.VMEM((sb_rows, d), BF16),
                        pltpu.VMEM((d, tf), BF16), pltpu.VMEM((d, tf), BF16), pltpu.VMEM((tf, d), BF16),
                        pltpu.SemaphoreType.DMA],
    )
    return pl.pallas_call(
        functools.partial(_expert_kernel, nf=nf),
        grid_spec=grid_spec,
        out_shape=jax.ShapeDtypeStruct(((n_sb + 1) * sb_rows, d), F32),
        compiler_params=_cparams(("arbitrary", "arbitrary"), VMEM_LIMIT),
        name="experts",
    )(sb_e, sb_n, n_used, row_tok.reshape(n_sb, 1, sb_rows), row_tok.reshape(n_sb, 1, sb_rows), xm, w_gu, w_gu, w_dn,
      b_gu.reshape(e, 1, f2), b_gu.reshape(e, 1, f2), b_dn.reshape(e, 1, d))


def _expert2_kernel(sbe_ref, sbn_ref, sbf_ref, nused_ref, tok_ref, tokn_ref, xm_hbm, wgu_hbm, wdn_hbm, bgu_ref, bdn_ref,
                    o_ref, xg_ref, xb_ref, wg_res, wl_res, wd_res, stg_g, stg_l, stg_d, gsem, wsem,
                    *, nf, tf, n_sb):
    i = pl.program_id(0)
    slot = i % 2
    e = sbe_ref[i]
    nsub = (sbn_ref[i] + MOE_SUB - 1) // MOE_SUB
    first = sbf_ref[i]
    sb_rows, d = xb_ref.shape
    subs = sb_rows // MOE_SUB
    step_rows = sb_rows // nf

    def weight_copies(expert, j, s):
        lo = pl.multiple_of(j * tf, tf)
        hi = pl.multiple_of(nf * tf + j * tf, tf)
        return (pltpu.make_async_copy(wgu_hbm.at[expert, :, pl.ds(lo, tf)], stg_g.at[s], wsem.at[0, s]),
                pltpu.make_async_copy(wgu_hbm.at[expert, :, pl.ds(hi, tf)], stg_l.at[s], wsem.at[1, s]),
                pltpu.make_async_copy(wdn_hbm.at[expert, pl.ds(lo, tf), :], stg_d.at[s], wsem.at[2, s]))

    def start_first_tiles(expert):
        for s in range(2):
            for cp in weight_copies(expert, s, s):
                cp.start()

    @pl.when(i == 0)
    def _():
        start_first_tiles(e)

        def issue(r8, c):
            for u in range(ISSUE_UNROLL):
                r = r8 * ISSUE_UNROLL + u
                _row_copy(xm_hbm, tok_ref[0, 0, r], xg_ref.at[0], r, gsem).start()
            return c

        lax.fori_loop(0, sb_rows // ISSUE_UNROLL, issue, 0)

    for s in range(subs):
        _slab_copy(xm_hbm, xg_ref.at[slot], s * MOE_SUB, MOE_SUB, gsem).wait()
    xb_ref[...] = xg_ref[slot].astype(BF16)
    o_ref[...] = jnp.broadcast_to(bdn_ref[...], o_ref.shape)

    def gather_next(j):
        for u in range(step_rows):
            r = j * step_rows + u
            _row_copy(xm_hbm, tokn_ref[0, 0, r], xg_ref.at[1 - slot], r, gsem).start()

    def ffn(j, rows):
        x = xb_ref[rows, :]
        g = jnp.minimum(_dot(x, wg_res[j]) + bgu_ref[j], SWIGLU_LIMIT)
        lin = jnp.clip(_dot(x, wl_res[j]) + bgu_ref[nf + j], -SWIGLU_LIMIT, SWIGLU_LIMIT)
        act = (g * jax.nn.sigmoid(SWIGLU_ALPHA * g) * (lin + 1.0)).astype(BF16)
        for c in range(0, d, MOE_NCHUNK):
            cols = pl.ds(c, min(MOE_NCHUNK, d - c))
            o_ref[rows, cols] = o_ref[rows, cols] + _dot(act, wd_res[j, :, cols])

    def tile(j, carry):
        s = j % 2

        @pl.when(first == 1)
        def _():
            for cp in weight_copies(e, j, s):
                cp.wait()
            wg_res[j] = stg_g[s].astype(BF16)
            wl_res[j] = stg_l[s].astype(BF16)
            wd_res[j] = stg_d[s].astype(BF16)

            @pl.when(j + 2 < nf)
            def _():
                for cp in weight_copies(e, j + 2, s):
                    cp.start()

        for k in range(subs + 1):
            @pl.when(nsub == k)
            def _():
                gather_next(j)
                if k:
                    ffn(j, pl.ds(0, k * MOE_SUB))

        return carry

    lax.fori_loop(0, nf, tile, 0)

    nxt = jnp.minimum(i + 1, n_sb - 1)

    @pl.when(jnp.logical_and(i + 1 < n_sb, sbf_ref[nxt] == 1))
    def _():
        start_first_tiles(sbe_ref[nxt])

    @pl.when(i == n_sb - 1)
    def _():
        for s in range(subs):
            _slab_copy(xm_hbm, xg_ref.at[1 - slot], s * MOE_SUB, MOE_SUB, gsem).wait()


def _experts2(xm, row_tok, sb_e, sb_n, sb_f, n_used, w_gu, b_gu, w_dn, b_dn):
    n_sb = sb_e.shape[0]
    sb_rows = MOE2_SB
    e, d, f2 = w_gu.shape
    f = f2 // 2
    tf = min(MOE_FTILE, f)
    nf = f // tf
    assert nf >= 2 and sb_rows % nf == 0
    tok3 = row_tok.reshape(n_sb, 1, sb_rows)
    grid_spec = pltpu.PrefetchScalarGridSpec(
        num_scalar_prefetch=4,
        grid=(n_sb,),
        in_specs=[
            pl.BlockSpec((1, 1, sb_rows), lambda i, sbe, sbn, sbf, nu: (i, 0, 0), memory_space=pltpu.SMEM),
            pl.BlockSpec((1, 1, sb_rows), lambda i, sbe, sbn, sbf, nu: (jnp.minimum(i + 1, n_sb - 1), 0, 0),
                         memory_space=pltpu.SMEM),
            pl.BlockSpec(memory_space=pl.ANY),
            pl.BlockSpec(memory_space=pl.ANY),
            pl.BlockSpec(memory_space=pl.ANY),
            pl.BlockSpec((None, 2 * nf, 1, tf), lambda i, sbe, sbn, sbf, nu: (sbe[i], 0, 0, 0)),
            pl.BlockSpec((None, 1, d), lambda i, sbe, sbn, sbf, nu: (sbe[i], 0, 0)),
        ],
        out_specs=pl.BlockSpec((sb_rows, d), lambda i, sbe, sbn, sbf, nu: (jnp.where(i < nu[0], i, n_sb), 0)),
        scratch_shapes=[pltpu.VMEM((2, sb_rows, d), F32), pltpu.VMEM((sb_rows, d), BF16),
                        pltpu.VMEM((nf, d, tf), BF16), pltpu.VMEM((nf, d, tf), BF16), pltpu.VMEM((nf, tf, d), BF16),
                        pltpu.VMEM((2, d, tf), F32), pltpu.VMEM((2, d, tf), F32), pltpu.VMEM((2, tf, d), F32),
                        pltpu.SemaphoreType.DMA, pltpu.SemaphoreType.DMA((3, 2))],
    )
    return pl.pallas_call(
        functools.partial(_expert2_kernel, nf=nf, tf=tf, n_sb=n_sb),
        grid_spec=grid_spec,
        out_shape=jax.ShapeDtypeStruct(((n_sb + 1) * sb_rows, d), F32),
        compiler_params=_cparams(("arbitrary",), MOE2_VMEM_LIMIT),
        name="experts",
    )(sb_e, sb_n, sb_f, n_used, tok3, tok3, xm, w_gu, w_dn, b_gu.reshape(e, 2 * nf, 1, tf), b_dn.reshape(e, 1, d))


def _combine_kernel(dest_ref, gate_ref, h_ref, g_ref, y_hbm, o_ref, buf_ref, sem):
    tm = h_ref.shape[0]

    tok_unroll = ISSUE_UNROLL // TOP_K

    def issue(t2, c):
        for u in range(tok_unroll):
            t = t2 * tok_unroll + u
            for k in range(TOP_K):
                _row_copy(y_hbm, dest_ref[0, 0, t * TOP_K + k], buf_ref.at[k], t, sem).start()
        return c

    lax.fori_loop(0, tm // tok_unroll, issue, 0)
    acc = h_ref[...]
    gates = gate_ref[...]
    for k in range(TOP_K):
        _slab_copy(y_hbm, buf_ref.at[k], 0, tm, sem).wait()
    for k in range(TOP_K):
        acc = acc + gates[:, k:k + 1] * buf_ref[k]
    o_ref[...] = _rms(acc, g_ref[...])


def _combine(dest, gates, h, g, yrows, tm):
    n, d = h.shape
    tm = min(tm, n)
    return pl.pallas_call(
        _combine_kernel,
        grid=(n // tm,),
        in_specs=[pl.BlockSpec((1, 1, tm * TOP_K), lambda i: (i, 0, 0), memory_space=pltpu.SMEM),
                  pl.BlockSpec((tm, LANES), lambda i: (i, 0)),
                  pl.BlockSpec((tm, d), lambda i: (i, 0)),
                  pl.BlockSpec((1, d), lambda i: (0, 0)),
                  pl.BlockSpec(memory_space=pl.ANY)],
        out_specs=pl.BlockSpec((tm, d), lambda i: (i, 0)),
        out_shape=jax.ShapeDtypeStruct((n, d), F32),
        scratch_shapes=[pltpu.VMEM((TOP_K, tm, d), F32), pltpu.SemaphoreType.DMA],
        compiler_params=_cparams(("arbitrary",), VMEM_LIMIT),
        name="combine",
    )(dest.reshape(n // tm, 1, tm * TOP_K), gates, h, g.reshape(1, d), yrows)


def _layer(h, mem2d, p, *, batch, length):
    n, d = h.shape
    g_cnt, p_state = p["lam_re"].shape
    sw = g_cnt * SSM_GROUP
    aw = d - sw
    heads = aw // ATTN_HEAD_DIM
    assert p["w_in"].shape[1] == sw + 3 * aw + heads and sw % LANES == 0 and 2 * p_state == LANES

    w_in = p["w_in"]
    cols = sw + 3 * aw
    q_scale = ATTN_HEAD_DIM ** -0.5 * LOG2E
    cs = jnp.concatenate([jnp.ones((sw,), F32), jnp.full((aw,), q_scale, F32), jnp.ones((2 * aw,), F32)])
    proj = _norm_matmul(h, p["g_mix"], w_in[:, :cols].astype(BF16), cs, BF16, 1024, 1024)
    w_f = jnp.pad(w_in[:, cols:], ((0, 0), (0, LANES - heads)))
    flog = _norm_matmul(h, p["g_mix"], w_f, jnp.ones((LANES,), F32), F32, 512, LANES, HIGHEST)

    f_rows = flog[:, :heads].reshape(batch, length, heads).transpose(0, 2, 1).reshape(batch * heads, length)
    bias_rows = jnp.tile(p["b_f"].astype(F32), batch).reshape(batch * heads, 1)
    cum_rows = _fgate_cumsum(f_rows, bias_rows)
    cum_col = cum_rows.reshape(batch, heads, length).transpose(0, 2, 1).reshape(n, heads)

    blk = lambda c: c // ATTN_HEAD_DIM
    att = _fox_attention(proj, cum_col, cum_rows.reshape(batch * heads, 1, length), batch=batch, length=length,
                         heads=heads, q_col=blk(sw), k_col=blk(sw + aw), v_col=blk(sw + 2 * aw))

    t = min(S5_CHUNK, length)
    chunks = length // t
    mi, ms, mo, a1, a2, dsk = _s5_params(p["lam_re"], p["lam_im"], p["log_step"], p["b_re"], p["b_im"],
                                         p["c_re"], p["c_im"], p["d_skip"], t, chunks)
    ug = proj[:, :sw].reshape(batch, chunks, t, g_cnt, SSM_GROUP).transpose(3, 0, 1, 2, 4)
    ug = ug.reshape(g_cnt, batch * chunks, t * SSM_GROUP)
    yg = _s5_mixer(ug, mi, ms, mo, a1, a2, dsk, chunks=chunks)
    ypre = yg.reshape(g_cnt, batch, chunks, t, SSM_GROUP).transpose(1, 2, 3, 0, 4).reshape(n, sw)

    w_out = p["w_out"].astype(BF16)
    h = _mix_out(ypre, att, h, p["w_glu"].astype(BF16), p["b_glu"], p["g_ssm_out"], p["g_attn_out"],
                 w_out[:aw], w_out[aw:], 256)

    mem_len = mem2d.shape[0] // batch
    kv = _norm_matmul(mem2d, p["g_mem"], p["w_ckv"].astype(BF16), jnp.ones((2 * d,), F32), BF16, 512, 512)
    h = _cross_attention(h, p["g_cross"], p["w_cq"].astype(BF16), kv, p["w_co"].astype(BF16),
                         batch=batch, mem_len=mem_len, tm=512)

    dest, gates, yrows = _moe(h, p)
    return dest, gates, h, yrows


def _moe(h, p):
    n = h.shape[0]
    n_exp = p["w_router"].shape[1]
    wr = jnp.pad(p["w_router"].astype(F32), ((0, 0), (0, LANES - n_exp)))
    br = jnp.pad(p["b_router"].astype(F32), (0, LANES - n_exp), constant_values=NEG_BIG).reshape(1, LANES)
    xm, idx, gates, rank, cnt = _router(h, p["g_moe"], wr, br, 256)

    sb_rows = MOE2_SB
    n_sb = (n * TOP_K) // sb_rows + n_exp
    counts = cnt[0, :n_exp]
    padded = (counts + sb_rows - 1) // sb_rows * sb_rows
    pend = jnp.cumsum(padded)
    pstart = pend - padded
    idx_k, rank_k = idx[:, :TOP_K], rank[:, :TOP_K]
    onehot = idx_k[..., None] == jnp.arange(n_exp, dtype=I32)
    dest = rank_k + jnp.sum(jnp.where(onehot, pstart, 0), axis=-1)
    tok = jnp.broadcast_to(jnp.arange(n, dtype=I32)[:, None], (n, TOP_K))
    row_tok = jnp.zeros((n_sb * sb_rows,), I32).at[dest.reshape(-1)].set(tok.reshape(-1), unique_indices=True)
    n_used = (pend[-1] // sb_rows).astype(I32)
    sb_start = jnp.arange(n_sb, dtype=I32) * sb_rows
    sb_e = jnp.clip(jnp.sum(sb_start[:, None] >= pend[None, :], axis=1), 0, n_exp - 1).astype(I32)
    used = jnp.arange(n_sb, dtype=I32) < n_used
    sb_n = jnp.where(used, jnp.clip(counts[sb_e] - (sb_start - pstart[sb_e]), 0, sb_rows), 0).astype(I32)
    sb_f = jnp.logical_and(used, sb_start == pstart[sb_e]).astype(I32)
    sb_e = jnp.where(used, sb_e, sb_e[jnp.maximum(n_used - 1, 0)])

    yrows = _experts2(xm, row_tok, sb_e, sb_n, sb_f, n_used.reshape(1), p["w_gu"], p["b_gu"], p["w_dn"], p["b_dn"])
    return dest, gates, yrows


def kernel(x, mem, g_mix, w_in, b_f, lam_re, lam_im, log_step, b_re, b_im, c_re, c_im, d_skip, w_glu, b_glu, g_attn_out, g_ssm_out, w_out, g_cross, g_mem, w_cq, w_ckv, w_co, g_moe, w_router, b_router, w_gu, b_gu, w_dn, b_dn, g_final):
    batch, length, d = x.shape
    depth = g_mix.shape[0]
    assert depth == 1, "the combine kernel applies the final norm, so exactly one layer is supported"
    names = ("g_mix", "w_in", "b_f", "lam_re", "lam_im", "log_step", "b_re", "b_im", "c_re", "c_im", "d_skip",
             "w_glu", "b_glu", "g_attn_out", "g_ssm_out", "w_out", "g_cross", "g_mem", "w_cq", "w_ckv", "w_co",
             "g_moe", "w_router", "b_router", "w_gu", "b_gu", "w_dn", "b_dn")
    vals = (g_mix, w_in, b_f, lam_re, lam_im, log_step, b_re, b_im, c_re, c_im, d_skip, w_glu, b_glu,
            g_attn_out, g_ssm_out, w_out, g_cross, g_mem, w_cq, w_ckv, w_co, g_moe, w_router, b_router,
            w_gu, b_gu, w_dn, b_dn)
    params = {k: v[0] for k, v in zip(names, vals)}
    h = x.reshape(batch * length, d)
    mem2d = mem.reshape(-1, d)
    dest, gates, h, yrows = _layer(h, mem2d, params, batch=batch, length=length)
    out = _combine(dest, gates, h, g_final, yrows, 256)
    return out.reshape(batch, length, d)
```

```python
import functools

import jax
import jax.numpy as jnp
from jax import lax
from jax.experimental import pallas as pl
from jax.experimental.pallas import tpu as pltpu

F32 = jnp.float32
BF16 = jnp.bfloat16
I32 = jnp.int32

RMS_EPS = 1e-5
SSM_GROUP = 16
ATTN_HEAD_DIM = 128
CROSS_HEADS = 4
TOP_K = 4
SWIGLU_ALPHA = 1.702
SWIGLU_LIMIT = 7.0

LANES = 128
LOG2E = 1.4426950408889634
ATT_BLOCK = 1024
ATT_SQ = 128
ATT_SK = 256
S5_CHUNK = 32
S5_REGROUP_ROWS = 64
MOE_SUB = 256
MOE_SUBS = 4
MOE_FTILE = 256
MOE_NCHUNK = 512
MOE2_SB = 512
MOE2_VMEM_LIMIT = 60 * 1024 * 1024
ISSUE_UNROLL = 8
NEG_BIG = -1e30
VMEM_LIMIT = 56 * 1024 * 1024

HIGHEST = lax.Precision.HIGHEST


def _cparams(sem, vmem=None):
    return pltpu.CompilerParams(dimension_semantics=sem, vmem_limit_bytes=vmem)


def _rms(x, g):
    return x * lax.rsqrt(jnp.mean(x * x, axis=-1, keepdims=True) + RMS_EPS) * g


def _dot(a, b, precision=None):
    return jnp.dot(a, b, preferred_element_type=F32, precision=precision)


def _dot_t(a, b):
    return lax.dot_general(a, b, (((1,), (1,)), ((), ())), preferred_element_type=F32)


def _norm_matmul_kernel(x_ref, g_ref, w_ref, cs_ref, o_ref, xn_ref, *, precision):
    @pl.when(pl.program_id(1) == 0)
    def _():
        xn_ref[...] = _rms(x_ref[...], g_ref[...]).astype(xn_ref.dtype)

    o_ref[...] = (_dot(xn_ref[...], w_ref[...], precision) * cs_ref[...]).astype(o_ref.dtype)


def _norm_matmul(x, g, w, col_scale, out_dtype, tm, tn, precision=None):
    n, k = x.shape
    m = w.shape[1]
    tm, tn = min(tm, n), min(tn, m)
    return pl.pallas_call(
        functools.partial(_norm_matmul_kernel, precision=precision),
        grid=(n // tm, m // tn),
        in_specs=[pl.BlockSpec((tm, k), lambda i, j: (i, 0)),
                  pl.BlockSpec((1, k), lambda i, j: (0, 0)),
                  pl.BlockSpec((k, tn), lambda i, j: (0, j)),
                  pl.BlockSpec((1, tn), lambda i, j: (0, j))],
        out_specs=pl.BlockSpec((tm, tn), lambda i, j: (i, j)),
        out_shape=jax.ShapeDtypeStruct((n, m), out_dtype),
        scratch_shapes=[pltpu.VMEM((tm, k), w.dtype)],
        compiler_params=_cparams(("parallel", "arbitrary"), VMEM_LIMIT),
        name="norm_matmul",
    )(x, g.reshape(1, k), w, col_scale.reshape(1, m))


def _fgate_cumsum_kernel(f_ref, b_ref, o_ref):
    z = f_ref[...] + b_ref[...]
    x = jnp.minimum(z, 0.0) - jnp.log(1.0 + jnp.exp(-jnp.abs(z)))
    lane = lax.broadcasted_iota(I32, x.shape, 1)
    d = 1
    while d < x.shape[1]:
        x = x + jnp.where(lane >= d, pltpu.roll(x, d, axis=1), 0.0)
        d *= 2
    o_ref[...] = x * LOG2E


def _fgate_cumsum(f_rows, bias_rows):
    return pl.pallas_call(
        _fgate_cumsum_kernel,
        out_shape=jax.ShapeDtypeStruct(f_rows.shape, F32),
        name="fgate_cumsum",
    )(f_rows, bias_rows)


def _fox_kernel(qi_ref, ki_ref, q_ref, k_ref, v_ref, cq_ref, ck_ref, o_ref, m_sc, l_sc, acc_sc, cq_sc,
                *, blk, sq, sk):
    h = pl.program_id(1)
    t = pl.program_id(2)
    qi = qi_ref[t]
    ki = ki_ref[t]
    rep = sk // LANES

    @pl.when(ki == 0)
    def _():
        m_sc[...] = jnp.full(m_sc.shape, -jnp.inf, F32)
        l_sc[...] = jnp.zeros(l_sc.shape, F32)
        acc_sc[...] = jnp.zeros(acc_sc.shape, F32)
        cq = cq_ref[...]
        lane = lax.broadcasted_iota(I32, cq.shape, 1)
        col = jnp.sum(jnp.where(lane == h, cq, 0.0), axis=1, keepdims=True)
        cq_sc[...] = jnp.broadcast_to(col, cq_sc.shape)

    def block(diag):
        for qs in range(blk // sq):
            r0 = qs * sq
            rows = pl.ds(r0, sq)
            q = q_ref[rows, :]
            cq = jnp.concatenate([cq_sc[rows, :]] * rep, axis=1)
            m, l, acc = m_sc[rows, :], l_sc[rows, :], acc_sc[rows, :]
            nkc = (r0 + sq - 1) // sk + 1 if diag else blk // sk
            for kc in range(nkc):
                c0 = kc * sk
                cols = pl.ds(c0, sk)
                s = _dot_t(q, k_ref[cols, :]) + cq - ck_ref[0, :, cols]
                if diag and c0 + sk - 1 > r0:
                    row = lax.broadcasted_iota(I32, s.shape, 0) + r0
                    col = lax.broadcasted_iota(I32, s.shape, 1) + c0
                    s = jnp.where(col <= row, s, -jnp.inf)
                m_new = jnp.maximum(m, jnp.max(s, axis=1, keepdims=True))
                alpha = jnp.exp2(m - m_new)
                p = jnp.exp2(s - jnp.concatenate([m_new] * rep, axis=1))
                psum = p[:, :LANES]
                for c in range(1, rep):
                    psum = psum + p[:, c * LANES:(c + 1) * LANES]
                l = alpha * l + psum
                acc = alpha * acc + _dot(p.astype(BF16), v_ref[cols, :])
                m = m_new
            if diag:
                o_ref[rows, :] = (acc / jnp.sum(l, axis=1, keepdims=True)).astype(o_ref.dtype)
            else:
                m_sc[rows, :], l_sc[rows, :], acc_sc[rows, :] = m, l, acc

    @pl.when(ki < qi)
    def _():
        block(False)

    @pl.when(ki == qi)
    def _():
        block(True)


def _fox_attention(proj, cum_col, cum_row, *, batch, length, heads, q_col, k_col, v_col):
    dh = ATTN_HEAD_DIM
    blk = min(ATT_BLOCK, length)
    sq, sk = min(ATT_SQ, blk), min(ATT_SK, blk)
    nq = length // blk
    n = batch * length
    pairs = [(a, b) for a in range(nq) for b in range(a + 1)]
    qi_tab = jnp.asarray([a for a, _ in pairs], I32)
    ki_tab = jnp.asarray([b for _, b in pairs], I32)
    kernel = functools.partial(_fox_kernel, blk=blk, sq=sq, sk=sk)
    grid_spec = pltpu.PrefetchScalarGridSpec(
        num_scalar_prefetch=2,
        grid=(batch, heads, len(pairs)),
        in_specs=[
            pl.BlockSpec((blk, dh), lambda b, h, t, qt, kt: (b * nq + qt[t], q_col + h)),
            pl.BlockSpec((blk, dh), lambda b, h, t, qt, kt: (b * nq + kt[t], k_col + h)),
            pl.BlockSpec((blk, dh), lambda b, h, t, qt, kt: (b * nq + kt[t], v_col + h)),
            pl.BlockSpec((blk, heads), lambda b, h, t, qt, kt: (b * nq + qt[t], 0)),
            pl.BlockSpec((1, 1, blk), lambda b, h, t, qt, kt: (b * heads + h, 0, kt[t])),
        ],
        out_specs=pl.BlockSpec((blk, dh), lambda b, h, t, qt, kt: (b * nq + qt[t], h)),
        scratch_shapes=[pltpu.VMEM((blk, LANES), F32), pltpu.VMEM((blk, LANES), F32),
                        pltpu.VMEM((blk, dh), F32), pltpu.VMEM((blk, LANES), F32)],
    )
    return pl.pallas_call(
        kernel,
        grid_spec=grid_spec,
        out_shape=jax.ShapeDtypeStruct((n, heads * dh), F32),
        compiler_params=_cparams(("parallel", "parallel", "arbitrary"), VMEM_LIMIT),
        name="fox_attention",
    )(qi_tab, ki_tab, proj, proj, proj, cum_col, cum_row)


def _s5_taps_kernel(wk_ref, cc_ref, o_ref):
    o_ref[0] = _dot(wk_ref[0], cc_ref[0], HIGHEST)


def _s5_kernel_taps(wk, cc):
    g, th, p2 = wk.shape
    h = cc.shape[2]
    return pl.pallas_call(
        _s5_taps_kernel,
        grid=(g,),
        in_specs=[pl.BlockSpec((1, th, p2), lambda i: (i, 0, 0)),
                  pl.BlockSpec((1, p2, h), lambda i: (i, 0, 0))],
        out_specs=pl.BlockSpec((1, th, h), lambda i: (i, 0, 0)),
        out_shape=jax.ShapeDtypeStruct((g, th, h), F32),
        compiler_params=_cparams(("parallel",)),
        name="s5_taps",
    )(wk, cc)


def _s5_mixer_kernel(u_ref, mi_ref, ms_ref, mo_ref, a1_ref, a2_ref, d_ref, y_ref, *, nlev, chunks, p):
    u = u_ref[0]
    s = _dot(u, ms_ref[0])
    row = lax.broadcasted_iota(I32, s.shape, 0) % chunks
    for j in range(nlev):
        d = 1 << j
        sh = jnp.where(row >= d, pltpu.roll(s, d, axis=0), 0.0)
        sw = pltpu.roll(sh, p, axis=1)
        s = s + a1_ref[0, j:j + 1, :] * sh + a2_ref[0, j:j + 1, :] * sw
    sp = jnp.where(row >= 1, pltpu.roll(s, 1, axis=0), 0.0)
    y = _dot(u, mi_ref[0]) + _dot(sp.astype(BF16), mo_ref[0]) + d_ref[0] * u.astype(F32)
    y_ref[0] = y


def _s5_mixer(ug, mi, ms, mo, a1, a2, dsk, *, chunks):
    g, r, th = ug.shape
    p2 = ms.shape[2]
    nlev = a1.shape[1]
    kernel = functools.partial(_s5_mixer_kernel, nlev=nlev, chunks=chunks, p=p2 // 2)
    return pl.pallas_call(
        kernel,
        grid=(g,),
        in_specs=[pl.BlockSpec((1, r, th), lambda i: (i, 0, 0)),
                  pl.BlockSpec((1, th, th), lambda i: (i, 0, 0)),
                  pl.BlockSpec((1, th, p2), lambda i: (i, 0, 0)),
                  pl.BlockSpec((1, p2, th), lambda i: (i, 0, 0)),
                  pl.BlockSpec((1, nlev, p2), lambda i: (i, 0, 0)),
                  pl.BlockSpec((1, nlev, p2), lambda i: (i, 0, 0)),
                  pl.BlockSpec((1, 1, th), lambda i: (i, 0, 0))],
        out_specs=pl.BlockSpec((1, r, th), lambda i: (i, 0, 0)),
        out_shape=jax.ShapeDtypeStruct((g, r, th), F32),
        compiler_params=_cparams(("parallel",), VMEM_LIMIT),
        name="s5_mixer",
    )(ug, mi, ms, mo, a1, a2, dsk)


def _s5_fused_kernel(u_ref, mi_ref, ms_ref, mo_ref, a1_ref, a2_ref, d_ref, y_ref, tmp_sc, us_sc,
                     *, nlev, chunks, p, t):
    h = SSM_GROUP
    gpb = LANES // h
    n = u_ref.shape[0]
    r = n // t
    nblk = t * h // LANES
    tpb = LANES // h
    rc = tmp_sc.shape[0] // t

    def regroup(c, carry):
        tmp_sc[...] = u_ref[pl.ds(pl.multiple_of(c * rc * t, rc * t), rc * t), :].astype(F32)
        for tau in range(t):
            us_sc[tau, pl.ds(pl.multiple_of(c * rc, rc), rc), :] = tmp_sc[pl.ds(tau, rc, stride=t), :]
        return carry

    lax.fori_loop(0, r // rc, regroup, 0)
    grp = lax.broadcasted_iota(I32, (r, LANES), 1) // h
    row = lax.broadcasted_iota(I32, (r, 2 * p), 0) % chunks

    def group(g, carry):
        blocks = []
        for ob in range(nblk):
            acc = jnp.zeros((r, LANES), F32)
            for q in range(tpb):
                s = us_sc[ob * tpb + q]
                acc = jnp.where(grp == q, pltpu.roll(s, ((q - g) * h) % LANES, axis=1), acc)
            blocks.append(acc)
        uf = jnp.concatenate(blocks, axis=1)
        u = uf.astype(BF16)
        s = _dot(u, ms_ref[g])
        for j in range(nlev):
            d = 1 << j
            sh = jnp.where(row >= d, pltpu.roll(s, d, axis=0), 0.0)
            sw = pltpu.roll(sh, p, axis=1)
            s = s + a1_ref[g, j:j + 1, :] * sh + a2_ref[g, j:j + 1, :] * sw
        sp = jnp.where(row >= 1, pltpu.roll(s, 1, axis=0), 0.0)
        y = _dot(u, mi_ref[g]) + _dot(sp.astype(BF16), mo_ref[g]) + d_ref[g] * u.astype(F32)
        def scatter_lanes(first):
            for ob in range(nblk):
                yb = y[:, ob * LANES:(ob + 1) * LANES]
                for q in range(tpb):
                    tau = ob * tpb + q
                    moved = pltpu.roll(yb, ((g - q) * h) % LANES, axis=1)
                    y_ref[tau] = jnp.where(grp == g, moved, 0.0 if first else y_ref[tau])

        @pl.when(g == 0)
        def _():
            scatter_lanes(True)

        @pl.when(g > 0)
        def _():
            scatter_lanes(False)

        return carry

    lax.fori_loop(0, gpb, group, 0)


def _s5_fused(u, mi, ms, mo, a1, a2, dsk, *, col_blocks, t, chunks):
    n = u.shape[0]
    gpb = LANES // SSM_GROUP
    th = t * SSM_GROUP
    p2 = ms.shape[2]
    nlev = a1.shape[1]
    r = n // t
    kernel = functools.partial(_s5_fused_kernel, nlev=nlev, chunks=chunks, p=p2 // 2, t=t)
    blk3 = lambda i: (i, 0, 0)
    return pl.pallas_call(
        kernel,
        grid=(col_blocks,),
        in_specs=[pl.BlockSpec((n, LANES), lambda i: (0, i)),
                  pl.BlockSpec((gpb, th, th), blk3),
                  pl.BlockSpec((gpb, th, p2), blk3),
                  pl.BlockSpec((gpb, p2, th), blk3),
                  pl.BlockSpec((gpb, nlev, p2), blk3),
                  pl.BlockSpec((gpb, nlev, p2), blk3),
                  pl.BlockSpec((gpb, 1, th), blk3)],
        out_specs=pl.BlockSpec((t, r, LANES), lambda i: (0, 0, i)),
        out_shape=jax.ShapeDtypeStruct((t, r, col_blocks * LANES), F32),
        scratch_shapes=[pltpu.VMEM((min(r, S5_REGROUP_ROWS) * t, LANES), F32), pltpu.VMEM((t, r, LANES), F32)],
        compiler_params=_cparams(("parallel",), VMEM_LIMIT),
        name="s5_mixer",
    )(u, mi, ms, mo, a1, a2, dsk)


def _s5_params(lam_re, lam_im, log_step, b_re, b_im, c_re, c_im, d_skip, t, chunks):
    g, p = lam_re.shape
    h = SSM_GROUP
    lr, li = lam_re.astype(F32), lam_im.astype(F32)
    dt = jnp.exp(log_step.astype(F32))[:, None]
    mag = jnp.exp(lr * dt)
    lb_re, lb_im = mag * jnp.cos(li * dt), mag * jnp.sin(li * dt)
    den = lr * lr + li * li
    nr, ni = lb_re - 1.0, lb_im
    coef_re = (nr * lr + ni * li) / den
    coef_im = (ni * lr - nr * li) / den
    br, bi = b_re.astype(F32), b_im.astype(F32)
    bb_re = coef_re[..., None] * br - coef_im[..., None] * bi
    bb_im = coef_re[..., None] * bi + coef_im[..., None] * br

    def power(k):
        return jnp.exp(lr * dt * k) * jnp.cos(li * dt * k), jnp.exp(lr * dt * k) * jnp.sin(li * dt * k)

    kk = jnp.arange(t + 1, dtype=F32)[:, None, None]
    pw_re, pw_im = power(kk)
    wb_re = pw_re[:t, ..., None] * bb_re - pw_im[:t, ..., None] * bb_im
    wb_im = pw_re[:t, ..., None] * bb_im + pw_im[:t, ..., None] * bb_re
    wk = jnp.concatenate([wb_re, wb_im], axis=2).transpose(1, 0, 3, 2)
    ms = wk[:, ::-1].reshape(g, t * h, 2 * p)
    cre, cim = c_re.astype(F32), c_im.astype(F32)
    cc = jnp.concatenate([cre.transpose(0, 2, 1), -cim.transpose(0, 2, 1)], axis=1)
    taps = _s5_kernel_taps(wk.reshape(g, t * h, 2 * p), cc)
    kt = taps.reshape(g, t, h, h).transpose(0, 2, 1, 3).reshape(g, h, t * h)
    kpad = jnp.concatenate([jnp.zeros_like(kt), kt], axis=2)
    mi = jnp.stack([kpad[:, :, (t - s) * h:(2 * t - s) * h] for s in range(t)], axis=1)
    mi = mi.reshape(g, t * h, t * h)
    cre_t, cim_t = cre.transpose(0, 2, 1)[:, :, None, :], cim.transpose(0, 2, 1)[:, :, None, :]
    are = pw_re[1:].transpose(1, 2, 0)[..., None]
    aim = pw_im[1:].transpose(1, 2, 0)[..., None]
    mo_re = cre_t * are - cim_t * aim
    mo_im = -(cre_t * aim + cim_t * are)
    mo = jnp.concatenate([mo_re, mo_im], axis=1).reshape(g, 2 * p, t * h)
    nlev = max(1, (chunks - 1).bit_length())
    lv = (float(t) * (2.0 ** jnp.arange(nlev, dtype=F32)))[None, :, None]
    ar, ai = power_levels(lr, li, dt, lv)
    a1 = jnp.concatenate([ar, ar], axis=2)
    a2 = jnp.concatenate([-ai, ai], axis=2)
    dsk = jnp.tile(d_skip.astype(F32).reshape(g, 1, h), (1, 1, t))
    return mi.astype(BF16), ms.astype(BF16), mo.astype(BF16), a1, a2, dsk


def power_levels(lr, li, dt, lv):
    e = (lr * dt)[:, None, :] * lv
    w = (li * dt)[:, None, :] * lv
    return jnp.exp(e) * jnp.cos(w), jnp.exp(e) * jnp.sin(w)


def _mix_out_kernel(y_ref, att_ref, x_ref, wglu_ref, bglu_ref, gs_ref, ga_ref, woa_ref, wos_ref, o_ref, y_sc):
    t = y_ref.shape[0]
    for rl in range(y_ref.shape[1]):
        y_sc[rl * t:(rl + 1) * t, :] = y_ref[:, rl, :]
    y = jax.nn.gelu(y_sc[...])
    z = _dot(y.astype(BF16), wglu_ref[...]) + bglu_ref[...]
    ssm = y * jax.nn.sigmoid(z)
    na = _rms(att_ref[...], ga_ref[...]).astype(BF16)
    ns = _rms(ssm, gs_ref[...]).astype(BF16)
    o_ref[...] = x_ref[...] + _dot(na, woa_ref[...]) + _dot(ns, wos_ref[...])


def _mix_out(ypre, att, x, w_glu, b_glu, g_ssm, g_att, wo_a, wo_s, tm):
    n, d = x.shape
    t, _, sw = ypre.shape
    aw = att.shape[1]
    tm = min(tm, n)
    row = lambda i: (i, 0)
    fix = lambda i: (0, 0)
    once = dict(pipeline_mode=pl.Buffered(1))
    return pl.pallas_call(
        _mix_out_kernel,
        grid=(n // tm,),
        in_specs=[pl.BlockSpec((t, tm // t, sw), lambda i: (0, i, 0)),
                  pl.BlockSpec((tm, aw), row), pl.BlockSpec((tm, d), row),
                  pl.BlockSpec((sw, sw), fix, **once), pl.BlockSpec((1, sw), fix),
                  pl.BlockSpec((1, sw), fix), pl.BlockSpec((1, aw), fix),
                  pl.BlockSpec((aw, d), fix, **once), pl.BlockSpec((sw, d), fix, **once)],
        out_specs=pl.BlockSpec((tm, d), row),
        out_shape=jax.ShapeDtypeStruct((n, d), F32),
        scratch_shapes=[pltpu.VMEM((tm, sw), F32)],
        compiler_params=_cparams(("parallel",), VMEM_LIMIT),
        name="mix_out",
    )(ypre, att, x, w_glu, b_glu.reshape(1, sw), g_ssm.reshape(1, sw), g_att.reshape(1, aw), wo_a, wo_s)


def _cross_kernel(h_ref, g_ref, wq_ref, kv_ref, wo_ref, o_ref, *, heads):
    h = h_ref[...]
    d = h.shape[1]
    dh = d // heads
    q = _dot(_rms(h, g_ref[...]).astype(BF16), wq_ref[...]).astype(BF16)
    out = h
    for hd in range(heads):
        lo = hd * dh
        s = _dot_t(q[:, lo:lo + dh], kv_ref[:, lo:lo + dh]) * dh ** -0.5
        e = jnp.exp(s - jnp.max(s, axis=1, keepdims=True))
        p = e / jnp.sum(e, axis=1, keepdims=True)
        oh = _dot(p.astype(BF16), kv_ref[:, d + lo:d + lo + dh])
        out = out + _dot(oh.astype(BF16), wo_ref[lo:lo + dh, :])
    o_ref[...] = out


def _cross_attention(h, g, wq, kv, wo, *, batch, mem_len, tm):
    n, d = h.shape
    tm = min(tm, n // batch)
    tiles = n // batch // tm
    fix = lambda i: (0, 0)
    once = dict(pipeline_mode=pl.Buffered(1))
    return pl.pallas_call(
        functools.partial(_cross_kernel, heads=CROSS_HEADS),
        grid=(n // tm,),
        in_specs=[pl.BlockSpec((tm, d), lambda i: (i, 0)),
                  pl.BlockSpec((1, d), fix),
                  pl.BlockSpec((d, d), fix, **once),
                  pl.BlockSpec((mem_len, 2 * d), lambda i: (i // tiles, 0)),
                  pl.BlockSpec((d, d), fix, **once)],
        out_specs=pl.BlockSpec((tm, d), lambda i: (i, 0)),
        out_shape=jax.ShapeDtypeStruct((n, d), F32),
        compiler_params=_cparams(("parallel",), VMEM_LIMIT),
        name="cross_attention",
    )(h, g.reshape(1, d), wq, kv, wo)


def _router_kernel(h_ref, g_ref, wr_ref, br_ref, xm_ref, idx_ref, gate_ref, rank_ref, cnt_ref, carry_sc):
    @pl.when(pl.program_id(0) == 0)
    def _():
        carry_sc[...] = jnp.zeros(carry_sc.shape, F32)

    hm = _rms(h_ref[...], g_ref[...])
    xm_ref[...] = hm
    logits = _dot(hm, wr_ref[...], HIGHEST) + br_ref[...]
    tm = logits.shape[0]
    lane = lax.broadcasted_iota(I32, logits.shape, 1).astype(F32)
    work = logits
    vals, idxs = [], []
    for _ in range(TOP_K):
        m = jnp.max(work, axis=1, keepdims=True)
        am = jnp.min(jnp.where(work == m, lane, float(LANES)), axis=1, keepdims=True)
        vals.append(m)
        idxs.append(am)
        work = jnp.where(lane == am, NEG_BIG, work)
    es = [jnp.exp(v - vals[0]) for v in vals]
    den = es[0] + es[1] + es[2] + es[3]
    sel = jnp.zeros(logits.shape, F32)
    for am in idxs:
        sel = sel + (lane == am).astype(F32)
    r = lax.broadcasted_iota(I32, (tm, tm), 0)
    c = lax.broadcasted_iota(I32, (tm, tm), 1)
    before = (c < r).astype(BF16)
    rank_all = carry_sc[0:1, :] + _dot(before, sel.astype(BF16))
    idx_o = jnp.zeros(logits.shape, F32)
    gate_o = jnp.zeros(logits.shape, F32)
    rank_o = jnp.zeros(logits.shape, F32)
    for j in range(TOP_K):
        slot = lane == float(j)
        rk = jnp.sum(jnp.where(lane == idxs[j], rank_all, 0.0), axis=1, keepdims=True)
        idx_o = jnp.where(slot, idxs[j], idx_o)
        gate_o = jnp.where(slot, es[j] / den, gate_o)
        rank_o = jnp.where(slot, rk, rank_o)
    idx_ref[...] = idx_o.astype(I32)
    gate_ref[...] = gate_o
    rank_ref[...] = rank_o.astype(I32)
    carry_sc[...] = carry_sc[...] + jnp.sum(sel, axis=0, keepdims=True)
    cnt_ref[...] = carry_sc[...].astype(I32)


def _router(h, g, wr, br, tm):
    n, d = h.shape
    tm = min(tm, n)
    row = lambda i: (i, 0)
    fix = lambda i: (0, 0)
    return pl.pallas_call(
        _router_kernel,
        grid=(n // tm,),
        in_specs=[pl.BlockSpec((tm, d), row), pl.BlockSpec((1, d), fix),
                  pl.BlockSpec((d, LANES), fix), pl.BlockSpec((1, LANES), fix)],
        out_specs=[pl.BlockSpec((tm, d), row), pl.BlockSpec((tm, LANES), row),
                   pl.BlockSpec((tm, LANES), row), pl.BlockSpec((tm, LANES), row),
                   pl.BlockSpec((8, LANES), fix)],
        out_shape=[jax.ShapeDtypeStruct((n, d), F32), jax.ShapeDtypeStruct((n, LANES), I32),
                   jax.ShapeDtypeStruct((n, LANES), F32), jax.ShapeDtypeStruct((n, LANES), I32),
                   jax.ShapeDtypeStruct((8, LANES), I32)],
        scratch_shapes=[pltpu.VMEM((8, LANES), F32)],
        compiler_params=_cparams(("arbitrary",), VMEM_LIMIT),
        name="router",
    )(h, g.reshape(1, d), wr, br)


def _row_copy(src_hbm, row, dst, slot, sem):
    return pltpu.make_async_copy(src_hbm.at[pl.ds(row, 1), :], dst.at[pl.ds(slot, 1), :], sem)


def _slab_copy(src_hbm, dst, slot, rows, sem):
    return pltpu.make_async_copy(src_hbm.at[pl.ds(0, rows), :], dst.at[pl.ds(slot, rows), :], sem)


def _expert_kernel(sbe_ref, sbn_ref, nused_ref, tok_ref, tokn_ref, xm_hbm, wg_ref, wl_ref, wd_ref, bg_ref,
                   bl_ref, bd_ref, o_ref, xg_ref, xb_ref, wgb_ref, wlb_ref, wdb_ref, sem, *, nf):
    i = pl.program_id(0)
    j = pl.program_id(1)
    n_i = pl.num_programs(0)
    sb_rows = xg_ref.shape[0]
    nrows = sbn_ref[i]
    nsub = (nrows + MOE_SUB - 1) // MOE_SUB

    def wait_rows():
        for s in range(MOE_SUBS):
            _slab_copy(xm_hbm, xg_ref, s * MOE_SUB, MOE_SUB, sem).wait()

    @pl.when(jnp.logical_and(i == 0, j == 0))
    def _():
        def issue(r8, c):
            for u in range(ISSUE_UNROLL):
                r = r8 * ISSUE_UNROLL + u
                _row_copy(xm_hbm, tok_ref[0, 0, r], xg_ref, r, sem).start()
            return c

        lax.fori_loop(0, sb_rows // ISSUE_UNROLL, issue, 0)

    @pl.when(j == 0)
    def _():
        for s in range(MOE_SUBS):
            rows = pl.ds(s * MOE_SUB, MOE_SUB)

            @pl.when(s < nsub)
            def _():
                o_ref[rows, :] = jnp.broadcast_to(bd_ref[...], (MOE_SUB, o_ref.shape[1]))

            @pl.when(s >= nsub)
            def _():
                o_ref[rows, :] = jnp.zeros((MOE_SUB, o_ref.shape[1]), F32)

        wait_rows()
        for s in range(MOE_SUBS):
            rows = pl.ds(s * MOE_SUB, MOE_SUB)

            @pl.when(s < nsub)
            def _():
                xb_ref[rows, :] = xg_ref[rows, :].astype(BF16)

    def prefetch_next():
        step_rows = sb_rows // nf
        for u in range(step_rows):
            r = j * step_rows + u
            _row_copy(xm_hbm, tokn_ref[0, 0, r], xg_ref, r, sem).start()

    def ffn(rows):
        x = xb_ref[rows, :]
        g = jnp.minimum(_dot(x, wgb_ref[...]) + bg_ref[...], SWIGLU_LIMIT)
        lin = jnp.clip(_dot(x, wlb_ref[...]) + bl_ref[...], -SWIGLU_LIMIT, SWIGLU_LIMIT)
        act = (g * jax.nn.sigmoid(SWIGLU_ALPHA * g) * (lin + 1.0)).astype(BF16)
        d = o_ref.shape[1]
        for c in range(0, d, MOE_NCHUNK):
            cols = pl.ds(c, min(MOE_NCHUNK, d - c))
            o_ref[rows, cols] = o_ref[rows, cols] + _dot(act, wdb_ref[:, cols])

    @pl.when(nrows > 0)
    def _():
        wgb_ref[...] = wg_ref[...].astype(BF16)
        wlb_ref[...] = wl_ref[...].astype(BF16)
        wdb_ref[...] = wd_ref[...].astype(BF16)

    @pl.when(nsub == MOE_SUBS)
    def _():
        prefetch_next()
        ffn(pl.ds(0, MOE_SUBS * MOE_SUB))

    @pl.when(nsub < MOE_SUBS)
    def _():
        prefetch_next()

    for s in range(MOE_SUBS - 1):
        @pl.when(jnp.logical_and(s < nsub, nsub < MOE_SUBS))
        def _():
            ffn(pl.ds(s * MOE_SUB, MOE_SUB))

    @pl.when(jnp.logical_and(i == n_i - 1, j == nf - 1))
    def _():
        wait_rows()


def _experts(xm, row_tok, sb_e, sb_n, n_used, w_gu, b_gu, w_dn, b_dn):
    n_sb = sb_e.shape[0]
    sb_rows = MOE_SUBS * MOE_SUB
    e, d, f2 = w_gu.shape
    f = f2 // 2
    tf = min(MOE_FTILE, f)
    nf = f // tf

    def jj(i, j, nused):
        return jnp.where(i < nused[0], j, nf - 1)

    grid_spec = pltpu.PrefetchScalarGridSpec(
        num_scalar_prefetch=3,
        grid=(n_sb, nf),
        in_specs=[
            pl.BlockSpec((1, 1, sb_rows), lambda i, j, sbe, sbn, nu: (i, 0, 0), memory_space=pltpu.SMEM),
            pl.BlockSpec((1, 1, sb_rows), lambda i, j, sbe, sbn, nu: (jnp.minimum(i + 1, n_sb - 1), 0, 0),
                         memory_space=pltpu.SMEM),
            pl.BlockSpec(memory_space=pl.ANY),
            pl.BlockSpec((None, d, tf), lambda i, j, sbe, sbn, nu: (sbe[i], 0, jj(i, j, nu))),
            pl.BlockSpec((None, d, tf), lambda i, j, sbe, sbn, nu: (sbe[i], 0, nf + jj(i, j, nu))),
            pl.BlockSpec((None, tf, d), lambda i, j, sbe, sbn, nu: (sbe[i], jj(i, j, nu), 0)),
            pl.BlockSpec((None, 1, tf), lambda i, j, sbe, sbn, nu: (sbe[i], 0, jj(i, j, nu))),
            pl.BlockSpec((None, 1, tf), lambda i, j, sbe, sbn, nu: (sbe[i], 0, nf + jj(i, j, nu))),
            pl.BlockSpec((None, 1, d), lambda i, j, sbe, sbn, nu: (sbe[i], 0, 0)),
        ],
        out_specs=pl.BlockSpec((sb_rows, d), lambda i, j, sbe, sbn, nu: (jnp.where(i < nu[0], i, n_sb), 0)),
        scratch_shapes=[pltpu.VMEM((sb_rows, d), F32), pltpu.VMEM((sb_rows, d), BF16),
                        pltpu.VMEM((d, tf), BF16), pltpu.VMEM((d, tf), BF16), pltpu.VMEM((tf, d), BF16),
                        pltpu.SemaphoreType.DMA],
    )
    return pl.pallas_call(
        functools.partial(_expert_kernel, nf=nf),
        grid_spec=grid_spec,
        out_shape=jax.ShapeDtypeStruct(((n_sb + 1) * sb_rows, d), F32),
        compiler_params=_cparams(("arbitrary", "arbitrary"), VMEM_LIMIT),
        name="experts",
    )(sb_e, sb_n, n_used, row_tok.reshape(n_sb, 1, sb_rows), row_tok.reshape(n_sb, 1, sb_rows), xm, w_gu, w_gu, w_dn,
      b_gu.reshape(e, 1, f2), b_gu.reshape(e, 1, f2), b_dn.reshape(e, 1, d))


def _expert2_kernel(sbe_ref, sbn_ref, sbf_ref, nused_ref, tok_ref, tokn_ref, xm_hbm, wgu_hbm, wdn_hbm, bgu_ref, bdn_ref,
                    o_ref, xg_ref, xb_ref, wg_res, wl_res, wd_res, stg_g, stg_l, stg_d, gsem, wsem,
                    *, nf, tf, n_sb):
    i = pl.program_id(0)
    slot = i % 2
    e = sbe_ref[i]
    nsub = (sbn_ref[i] + MOE_SUB - 1) // MOE_SUB
    first = sbf_ref[i]
    sb_rows, d = xb_ref.shape
    subs = sb_rows // MOE_SUB
    step_rows = sb_rows // nf

    def weight_copies(expert, j, s):
        lo = pl.multiple_of(j * tf, tf)
        hi = pl.multiple_of(nf * tf + j * tf, tf)
        return (pltpu.make_async_copy(wgu_hbm.at[expert, :, pl.ds(lo, tf)], stg_g.at[s], wsem.at[0, s]),
                pltpu.make_async_copy(wgu_hbm.at[expert, :, pl.ds(hi, tf)], stg_l.at[s], wsem.at[1, s]),
                pltpu.make_async_copy(wdn_hbm.at[expert, pl.ds(lo, tf), :], stg_d.at[s], wsem.at[2, s]))

    def start_first_tiles(expert):
        for s in range(2):
            for cp in weight_copies(expert, s, s):
                cp.start()

    @pl.when(i == 0)
    def _():
        start_first_tiles(e)

        def issue(r8, c):
            for u in range(ISSUE_UNROLL):
                r = r8 * ISSUE_UNROLL + u
                _row_copy(xm_hbm, tok_ref[0, 0, r], xg_ref.at[0], r, gsem).start()
            return c

        lax.fori_loop(0, sb_rows // ISSUE_UNROLL, issue, 0)

    for s in range(subs):
        _slab_copy(xm_hbm, xg_ref.at[slot], s * MOE_SUB, MOE_SUB, gsem).wait()
    xb_ref[...] = xg_ref[slot].astype(BF16)
    o_ref[...] = jnp.broadcast_to(bdn_ref[...], o_ref.shape)

    def gather_next(j):
        for u in range(step_rows):
            r = j * step_rows + u
            _row_copy(xm_hbm, tokn_ref[0, 0, r], xg_ref.at[1 - slot], r, gsem).start()

    def ffn(j, rows):
        x = xb_ref[rows, :]
        g = jnp.minimum(_dot(x, wg_res[j]) + bgu_ref[j], SWIGLU_LIMIT)
        lin = jnp.clip(_dot(x, wl_res[j]) + bgu_ref[nf + j], -SWIGLU_LIMIT, SWIGLU_LIMIT)
        act = (g * jax.nn.sigmoid(SWIGLU_ALPHA * g) * (lin + 1.0)).astype(BF16)
        for c in range(0, d, MOE_NCHUNK):
            cols = pl.ds(c, min(MOE_NCHUNK, d - c))
            o_ref[rows, cols] = o_ref[rows, cols] + _dot(act, wd_res[j, :, cols])

    def tile(j, carry):
        s = j % 2

        @pl.when(first == 1)
        def _():
            for cp in weight_copies(e, j, s):
                cp.wait()
            wg_res[j] = stg_g[s].astype(BF16)
            wl_res[j] = stg_l[s].astype(BF16)
            wd_res[j] = stg_d[s].astype(BF16)

            @pl.when(j + 2 < nf)
            def _():
                for cp in weight_copies(e, j + 2, s):
                    cp.start()

        for k in range(subs + 1):
            @pl.when(nsub == k)
            def _():
                gather_next(j)
                if k:
                    ffn(j, pl.ds(0, k * MOE_SUB))

        return carry

    lax.fori_loop(0, nf, tile, 0)

    nxt = jnp.minimum(i + 1, n_sb - 1)

    @pl.when(jnp.logical_and(i + 1 < n_sb, sbf_ref[nxt] == 1))
    def _():
        start_first_tiles(sbe_ref[nxt])

    @pl.when(i == n_sb - 1)
    def _():
        for s in range(subs):
            _slab_copy(xm_hbm, xg_ref.at[1 - slot], s * MOE_SUB, MOE_SUB, gsem).wait()


def _experts2(xm, row_tok, sb_e, sb_n, sb_f, n_used, w_gu, b_gu, w_dn, b_dn):
    n_sb = sb_e.shape[0]
    sb_rows = MOE2_SB
    e, d, f2 = w_gu.shape
    f = f2 // 2
    tf = min(MOE_FTILE, f)
    nf = f // tf
    assert nf >= 2 and sb_rows % nf == 0
    tok3 = row_tok.reshape(n_sb, 1, sb_rows)
    grid_spec = pltpu.PrefetchScalarGridSpec(
        num_scalar_prefetch=4,
        grid=(n_sb,),
        in_specs=[
            pl.BlockSpec((1, 1, sb_rows), lambda i, sbe, sbn, sbf, nu: (i, 0, 0), memory_space=pltpu.SMEM),
            pl.BlockSpec((1, 1, sb_rows), lambda i, sbe, sbn, sbf, nu: (jnp.minimum(i + 1, n_sb - 1), 0, 0),
                         memory_space=pltpu.SMEM),
            pl.BlockSpec(memory_space=pl.ANY),
            pl.BlockSpec(memory_space=pl.ANY),
            pl.BlockSpec(memory_space=pl.ANY),
            pl.BlockSpec((None, 2 * nf, 1, tf), lambda i, sbe, sbn, sbf, nu: (sbe[i], 0, 0, 0)),
            pl.BlockSpec((None, 1, d), lambda i, sbe, sbn, sbf, nu: (sbe[i], 0, 0)),
        ],
        out_specs=pl.BlockSpec((sb_rows, d), lambda i, sbe, sbn, sbf, nu: (jnp.where(i < nu[0], i, n_sb), 0)),
        scratch_shapes=[pltpu.VMEM((2, sb_rows, d), F32), pltpu.VMEM((sb_rows, d), BF16),
                        pltpu.VMEM((nf, d, tf), BF16), pltpu.VMEM((nf, d, tf), BF16), pltpu.VMEM((nf, tf, d), BF16),
                        pltpu.VMEM((2, d, tf), F32), pltpu.VMEM((2, d, tf), F32), pltpu.VMEM((2, tf, d), F32),
                        pltpu.SemaphoreType.DMA, pltpu.SemaphoreType.DMA((3, 2))],
    )
    return pl.pallas_call(
        functools.partial(_expert2_kernel, nf=nf, tf=tf, n_sb=n_sb),
        grid_spec=grid_spec,
        out_shape=jax.ShapeDtypeStruct(((n_sb + 1) * sb_rows, d), F32),
        compiler_params=_cparams(("arbitrary",), MOE2_VMEM_LIMIT),
        name="experts",
    )(sb_e, sb_n, sb_f, n_used, tok3, tok3, xm, w_gu, w_dn, b_gu.reshape(e, 2 * nf, 1, tf), b_dn.reshape(e, 1, d))


def _combine_kernel(dest_ref, gate_ref, h_ref, g_ref, y_hbm, o_ref, buf_ref, sem):
    tm = h_ref.shape[0]

    tok_unroll = ISSUE_UNROLL // TOP_K

    def issue(t2, c):
        for u in range(tok_unroll):
            t = t2 * tok_unroll + u
            for k in range(TOP_K):
                _row_copy(y_hbm, dest_ref[0, 0, t * TOP_K + k], buf_ref.at[k], t, sem).start()
        return c

    lax.fori_loop(0, tm // tok_unroll, issue, 0)
    acc = h_ref[...]
    gates = gate_ref[...]
    for k in range(TOP_K):
        _slab_copy(y_hbm, buf_ref.at[k], 0, tm, sem).wait()
    for k in range(TOP_K):
        acc = acc + gates[:, k:k + 1] * buf_ref[k]
    o_ref[...] = _rms(acc, g_ref[...])


def _combine(dest, gates, h, g, yrows, tm):
    n, d = h.shape
    tm = min(tm, n)
    return pl.pallas_call(
        _combine_kernel,
        grid=(n // tm,),
        in_specs=[pl.BlockSpec((1, 1, tm * TOP_K), lambda i: (i, 0, 0), memory_space=pltpu.SMEM),
                  pl.BlockSpec((tm, LANES), lambda i: (i, 0)),
                  pl.BlockSpec((tm, d), lambda i: (i, 0)),
                  pl.BlockSpec((1, d), lambda i: (0, 0)),
                  pl.BlockSpec(memory_space=pl.ANY)],
        out_specs=pl.BlockSpec((tm, d), lambda i: (i, 0)),
        out_shape=jax.ShapeDtypeStruct((n, d), F32),
        scratch_shapes=[pltpu.VMEM((TOP_K, tm, d), F32), pltpu.SemaphoreType.DMA],
        compiler_params=_cparams(("arbitrary",), VMEM_LIMIT),
        name="combine",
    )(dest.reshape(n // tm, 1, tm * TOP_K), gates, h, g.reshape(1, d), yrows)


def _layer(h, mem2d, p, *, batch, length):
    n, d = h.shape
    g_cnt, p_state = p["lam_re"].shape
    sw = g_cnt * SSM_GROUP
    aw = d - sw
    heads = aw // ATTN_HEAD_DIM
    assert p["w_in"].shape[1] == sw + 3 * aw + heads and sw % LANES == 0 and 2 * p_state == LANES

    w_in = p["w_in"]
    cols = sw + 3 * aw
    q_scale = ATTN_HEAD_DIM ** -0.5 * LOG2E
    cs = jnp.concatenate([jnp.ones((sw,), F32), jnp.full((aw,), q_scale, F32), jnp.ones((2 * aw,), F32)])
    proj = _norm_matmul(h, p["g_mix"], w_in[:, :cols].astype(BF16), cs, BF16, 1024, 1024)
    w_f = jnp.pad(w_in[:, cols:], ((0, 0), (0, LANES - heads)))
    flog = _norm_matmul(h, p["g_mix"], w_f, jnp.ones((LANES,), F32), F32, 512, LANES, HIGHEST)

    f_rows = flog[:, :heads].reshape(batch, length, heads).transpose(0, 2, 1).reshape(batch * heads, length)
    bias_rows = jnp.tile(p["b_f"].astype(F32), batch).reshape(batch * heads, 1)
    cum_rows = _fgate_cumsum(f_rows, bias_rows)
    cum_col = cum_rows.reshape(batch, heads, length).transpose(0, 2, 1).reshape(n, heads)

    blk = lambda c: c // ATTN_HEAD_DIM
    att = _fox_attention(proj, cum_col, cum_rows.reshape(batch * heads, 1, length), batch=batch, length=length,
                         heads=heads, q_col=blk(sw), k_col=blk(sw + aw), v_col=blk(sw + 2 * aw))

    t = min(S5_CHUNK, length)
    chunks = length // t
    mi, ms, mo, a1, a2, dsk = _s5_params(p["lam_re"], p["lam_im"], p["log_step"], p["b_re"], p["b_im"],
                                         p["c_re"], p["c_im"], p["d_skip"], t, chunks)
    ypre = _s5_fused(proj, mi, ms, mo, a1, a2, dsk, col_blocks=sw // LANES, t=t, chunks=chunks)

    w_out = p["w_out"].astype(BF16)
    h = _mix_out(ypre, att, h, p["w_glu"].astype(BF16), p["b_glu"], p["g_ssm_out"], p["g_attn_out"],
                 w_out[:aw], w_out[aw:], 256)

    mem_len = mem2d.shape[0] // batch
    kv = _norm_matmul(mem2d, p["g_mem"], p["w_ckv"].astype(BF16), jnp.ones((2 * d,), F32), BF16, 512, 512)
    h = _cross_attention(h, p["g_cross"], p["w_cq"].astype(BF16), kv, p["w_co"].astype(BF16),
                         batch=batch, mem_len=mem_len, tm=512)

    dest, gates, yrows = _moe(h, p)
    return dest, gates, h, yrows


def _moe(h, p):
    n = h.shape[0]
    n_exp = p["w_router"].shape[1]
    wr = jnp.pad(p["w_router"].astype(F32), ((0, 0), (0, LANES - n_exp)))
    br = jnp.pad(p["b_router"].astype(F32), (0, LANES - n_exp), constant_values=NEG_BIG).reshape(1, LANES)
    xm, idx, gates, rank, cnt = _router(h, p["g_moe"], wr, br, 256)

    sb_rows = MOE2_SB
    n_sb = (n * TOP_K) // sb_rows + n_exp
    counts = cnt[0, :n_exp]
    padded = (counts + sb_rows - 1) // sb_rows * sb_rows
    pend = jnp.cumsum(padded)
    pstart = pend - padded
    idx_k, rank_k = idx[:, :TOP_K], rank[:, :TOP_K]
    onehot = idx_k[..., None] == jnp.arange(n_exp, dtype=I32)
    dest = rank_k + jnp.sum(jnp.where(onehot, pstart, 0), axis=-1)
    tok = jnp.broadcast_to(jnp.arange(n, dtype=I32)[:, None], (n, TOP_K))
    row_tok = jnp.zeros((n_sb * sb_rows,), I32).at[dest.reshape(-1)].set(tok.reshape(-1), unique_indices=True)
    n_used = (pend[-1] // sb_rows).astype(I32)
    sb_start = jnp.arange(n_sb, dtype=I32) * sb_rows
    sb_e = jnp.clip(jnp.sum(sb_start[:, None] >= pend[None, :], axis=1), 0, n_exp - 1).astype(I32)
    used = jnp.arange(n_sb, dtype=I32) < n_used
    sb_n = jnp.where(used, jnp.clip(counts[sb_e] - (sb_start - pstart[sb_e]), 0, sb_rows), 0).astype(I32)
    sb_f = jnp.logical_and(used, sb_start == pstart[sb_e]).astype(I32)
    sb_e = jnp.where(used, sb_e, sb_e[jnp.maximum(n_used - 1, 0)])

    yrows = _experts2(xm, row_tok, sb_e, sb_n, sb_f, n_used.reshape(1), p["w_gu"], p["b_gu"], p["w_dn"], p["b_dn"])
    return dest, gates, yrows


def kernel(x, mem, g_mix, w_in, b_f, lam_re, lam_im, log_step, b_re, b_im, c_re, c_im, d_skip, w_glu, b_glu, g_attn_out, g_ssm_out, w_out, g_cross, g_mem, w_cq, w_ckv, w_co, g_moe, w_router, b_router, w_gu, b_gu, w_dn, b_dn, g_final):
    batch, length, d = x.shape
    depth = g_mix.shape[0]
    assert depth == 1, "the combine kernel applies the final norm, so exactly one layer is supported"
    names = ("g_mix", "w_in", "b_f", "lam_re", "lam_im", "log_step", "b_re", "b_im", "c_re", "c_im", "d_skip",
             "w_glu", "b_glu", "g_attn_out", "g_ssm_out", "w_out", "g_cross", "g_mem", "w_cq", "w_ckv", "w_co",
             "g_moe", "w_router", "b_router", "w_gu", "b_gu", "w_dn", "b_dn")
    vals = (g_mix, w_in, b_f, lam_re, lam_im, log_step, b_re, b_im, c_re, c_im, d_skip, w_glu, b_glu,
            g_attn_out, g_ssm_out, w_out, g_cross, g_mem, w_cq, w_ckv, w_co, g_moe, w_router, b_router,
            w_gu, b_gu, w_dn, b_dn)
    params = {k: v[0] for k, v in zip(names, vals)}
    h = x.reshape(batch * length, d)
    mem2d = mem.reshape(-1, d)
    dest, gates, h, yrows = _layer(h, mem2d, params, batch=batch, length=length)
    out = _combine(dest, gates, h, g_final, yrows, 256)
    return out.reshape(batch, length, d)
```

```python
import functools

import jax
import jax.numpy as jnp
from jax import lax
from jax.experimental import pallas as pl
from jax.experimental.pallas import tpu as pltpu

F32 = jnp.float32
BF16 = jnp.bfloat16
I32 = jnp.int32

RMS_EPS = 1e-5
SSM_GROUP = 16
ATTN_HEAD_DIM = 128
CROSS_HEADS = 4
TOP_K = 4
SWIGLU_ALPHA = 1.702
SWIGLU_LIMIT = 7.0

LANES = 128
LOG2E = 1.4426950408889634
ATT_BLOCK = 1024
ATT_SQ = 128
ATT_SK = 256
S5_CHUNK = 32
S5_REGROUP_ROWS = 64
MOE_SUB = 256
MOE_SUBS = 4
MOE_FTILE = 256
MOE_NCHUNK = 512
MOE2_SB = 512
MOE2_VMEM_LIMIT = 60 * 1024 * 1024
ISSUE_UNROLL = 8
NEG_BIG = -1e30
VMEM_LIMIT = 56 * 1024 * 1024

HIGHEST = lax.Precision.HIGHEST


def _cparams(sem, vmem=None):
    return pltpu.CompilerParams(dimension_semantics=sem, vmem_limit_bytes=vmem)


def _rms(x, g):
    return x * lax.rsqrt(jnp.mean(x * x, axis=-1, keepdims=True) + RMS_EPS) * g


def _dot(a, b, precision=None):
    return jnp.dot(a, b, preferred_element_type=F32, precision=precision)


def _dot_t(a, b):
    return lax.dot_general(a, b, (((1,), (1,)), ((), ())), preferred_element_type=F32)


def _norm_matmul_kernel(x_ref, g_ref, w_ref, cs_ref, o_ref, xn_ref, *, precision):
    @pl.when(pl.program_id(1) == 0)
    def _():
        xn_ref[...] = _rms(x_ref[...], g_ref[...]).astype(xn_ref.dtype)

    o_ref[...] = (_dot(xn_ref[...], w_ref[...], precision) * cs_ref[...]).astype(o_ref.dtype)


def _norm_matmul(x, g, w, col_scale, out_dtype, tm, tn, precision=None):
    n, k = x.shape
    m = w.shape[1]
    tm, tn = min(tm, n), min(tn, m)
    return pl.pallas_call(
        functools.partial(_norm_matmul_kernel, precision=precision),
        grid=(n // tm, m // tn),
        in_specs=[pl.BlockSpec((tm, k), lambda i, j: (i, 0)),
                  pl.BlockSpec((1, k), lambda i, j: (0, 0)),
                  pl.BlockSpec((k, tn), lambda i, j: (0, j)),
                  pl.BlockSpec((1, tn), lambda i, j: (0, j))],
        out_specs=pl.BlockSpec((tm, tn), lambda i, j: (i, j)),
        out_shape=jax.ShapeDtypeStruct((n, m), out_dtype),
        scratch_shapes=[pltpu.VMEM((tm, k), w.dtype)],
        compiler_params=_cparams(("parallel", "arbitrary"), VMEM_LIMIT),
        name="norm_matmul",
    )(x, g.reshape(1, k), w, col_scale.reshape(1, m))


def _in_proj_kernel(x_ref, g_ref, w_ref, cs_ref, wf_ref, o_ref, f_ref, xn_ref):
    @pl.when(pl.program_id(1) == 0)
    def _():
        xn = _rms(x_ref[...], g_ref[...])
        xn_ref[...] = xn.astype(xn_ref.dtype)
        f_ref[...] = _dot(xn, wf_ref[...], HIGHEST)

    o_ref[...] = (_dot(xn_ref[...], w_ref[...]) * cs_ref[...]).astype(o_ref.dtype)


def _in_proj(x, g, w, col_scale, w_f, tm, tn):
    n, k = x.shape
    m = w.shape[1]
    tm, tn = min(tm, n), min(tn, m)
    return pl.pallas_call(
        _in_proj_kernel,
        grid=(n // tm, m // tn),
        in_specs=[pl.BlockSpec((tm, k), lambda i, j: (i, 0)),
                  pl.BlockSpec((1, k), lambda i, j: (0, 0)),
                  pl.BlockSpec((k, tn), lambda i, j: (0, j)),
                  pl.BlockSpec((1, tn), lambda i, j: (0, j)),
                  pl.BlockSpec((k, LANES), lambda i, j: (0, 0))],
        out_specs=[pl.BlockSpec((tm, tn), lambda i, j: (i, j)),
                   pl.BlockSpec((tm, LANES), lambda i, j: (i, 0))],
        out_shape=[jax.ShapeDtypeStruct((n, m), BF16), jax.ShapeDtypeStruct((n, LANES), F32)],
        scratch_shapes=[pltpu.VMEM((tm, k), BF16)],
        compiler_params=_cparams(("parallel", "arbitrary"), VMEM_LIMIT),
        name="in_proj",
    )(x, g.reshape(1, k), w, col_scale.reshape(1, m), w_f)


def _fgate_cumsum_kernel(f_ref, b_ref, o_ref):
    z = f_ref[...] + b_ref[...]
    x = jnp.minimum(z, 0.0) - jnp.log(1.0 + jnp.exp(-jnp.abs(z)))
    lane = lax.broadcasted_iota(I32, x.shape, 1)
    d = 1
    while d < x.shape[1]:
        x = x + jnp.where(lane >= d, pltpu.roll(x, d, axis=1), 0.0)
        d *= 2
    o_ref[...] = x * LOG2E


def _fgate_cumsum(f_rows, bias_rows):
    return pl.pallas_call(
        _fgate_cumsum_kernel,
        out_shape=jax.ShapeDtypeStruct(f_rows.shape, F32),
        name="fgate_cumsum",
    )(f_rows, bias_rows)


def _fox_kernel(qi_ref, ki_ref, q_ref, k_ref, v_ref, cq_ref, ck_ref, o_ref, m_sc, l_sc, acc_sc, cq_sc,
                *, blk, sq, sk):
    h = pl.program_id(1)
    t = pl.program_id(2)
    qi = qi_ref[t]
    ki = ki_ref[t]
    rep = sk // LANES

    @pl.when(ki == 0)
    def _():
        m_sc[...] = jnp.full(m_sc.shape, -jnp.inf, F32)
        l_sc[...] = jnp.zeros(l_sc.shape, F32)
        acc_sc[...] = jnp.zeros(acc_sc.shape, F32)
        cq = cq_ref[...]
        lane = lax.broadcasted_iota(I32, cq.shape, 1)
        col = jnp.sum(jnp.where(lane == h, cq, 0.0), axis=1, keepdims=True)
        cq_sc[...] = jnp.broadcast_to(col, cq_sc.shape)

    def block(diag):
        for qs in range(blk // sq):
            r0 = qs * sq
            rows = pl.ds(r0, sq)
            q = q_ref[rows, :]
            cq = jnp.concatenate([cq_sc[rows, :]] * rep, axis=1)
            m, l, acc = m_sc[rows, :], l_sc[rows, :], acc_sc[rows, :]
            nkc = (r0 + sq - 1) // sk + 1 if diag else blk // sk
            for kc in range(nkc):
                c0 = kc * sk
                cols = pl.ds(c0, sk)
                s = _dot_t(q, k_ref[cols, :]) + cq - ck_ref[0, :, cols]
                if diag and c0 + sk - 1 > r0:
                    row = lax.broadcasted_iota(I32, s.shape, 0) + r0
                    col = lax.broadcasted_iota(I32, s.shape, 1) + c0
                    s = jnp.where(col <= row, s, -jnp.inf)
                m_new = jnp.maximum(m, jnp.max(s, axis=1, keepdims=True))
                alpha = jnp.exp2(m - m_new)
                p = jnp.exp2(s - jnp.concatenate([m_new] * rep, axis=1))
                psum = p[:, :LANES]
                for c in range(1, rep):
                    psum = psum + p[:, c * LANES:(c + 1) * LANES]
                l = alpha * l + psum
                acc = alpha * acc + _dot(p.astype(BF16), v_ref[cols, :])
                m = m_new
            if diag:
                o_ref[rows, :] = (acc / jnp.sum(l, axis=1, keepdims=True)).astype(o_ref.dtype)
            else:
                m_sc[rows, :], l_sc[rows, :], acc_sc[rows, :] = m, l, acc

    @pl.when(ki < qi)
    def _():
        block(False)

    @pl.when(ki == qi)
    def _():
        block(True)


def _fox_attention(proj, cum_col, cum_row, *, batch, length, heads, q_col, k_col, v_col):
    dh = ATTN_HEAD_DIM
    blk = min(ATT_BLOCK, length)
    sq, sk = min(ATT_SQ, blk), min(ATT_SK, blk)
    nq = length // blk
    n = batch * length
    pairs = [(a, b) for a in range(nq) for b in range(a + 1)]
    qi_tab = jnp.asarray([a for a, _ in pairs], I32)
    ki_tab = jnp.asarray([b for _, b in pairs], I32)
    kernel = functools.partial(_fox_kernel, blk=blk, sq=sq, sk=sk)
    grid_spec = pltpu.PrefetchScalarGridSpec(
        num_scalar_prefetch=2,
        grid=(batch, heads, len(pairs)),
        in_specs=[
            pl.BlockSpec((blk, dh), lambda b, h, t, qt, kt: (b * nq + qt[t], q_col + h)),
            pl.BlockSpec((blk, dh), lambda b, h, t, qt, kt: (b * nq + kt[t], k_col + h)),
            pl.BlockSpec((blk, dh), lambda b, h, t, qt, kt: (b * nq + kt[t], v_col + h)),
            pl.BlockSpec((blk, heads), lambda b, h, t, qt, kt: (b * nq + qt[t], 0)),
            pl.BlockSpec((1, 1, blk), lambda b, h, t, qt, kt: (b * heads + h, 0, kt[t])),
        ],
        out_specs=pl.BlockSpec((blk, dh), lambda b, h, t, qt, kt: (b * nq + qt[t], h)),
        scratch_shapes=[pltpu.VMEM((blk, LANES), F32), pltpu.VMEM((blk, LANES), F32),
                        pltpu.VMEM((blk, dh), F32), pltpu.VMEM((blk, LANES), F32)],
    )
    return pl.pallas_call(
        kernel,
        grid_spec=grid_spec,
        out_shape=jax.ShapeDtypeStruct((n, heads * dh), F32),
        compiler_params=_cparams(("parallel", "parallel", "arbitrary"), VMEM_LIMIT),
        name="fox_attention",
    )(qi_tab, ki_tab, proj, proj, proj, cum_col, cum_row)


def _s5_taps_kernel(wk_ref, cc_ref, o_ref):
    o_ref[0] = _dot(wk_ref[0], cc_ref[0], HIGHEST)


def _s5_kernel_taps(wk, cc):
    g, th, p2 = wk.shape
    h = cc.shape[2]
    return pl.pallas_call(
        _s5_taps_kernel,
        grid=(g,),
        in_specs=[pl.BlockSpec((1, th, p2), lambda i: (i, 0, 0)),
                  pl.BlockSpec((1, p2, h), lambda i: (i, 0, 0))],
        out_specs=pl.BlockSpec((1, th, h), lambda i: (i, 0, 0)),
        out_shape=jax.ShapeDtypeStruct((g, th, h), F32),
        compiler_params=_cparams(("parallel",)),
        name="s5_taps",
    )(wk, cc)


def _s5_mixer_kernel(u_ref, mi_ref, ms_ref, mo_ref, a1_ref, a2_ref, d_ref, y_ref, *, nlev, chunks, p):
    u = u_ref[0]
    s = _dot(u, ms_ref[0])
    row = lax.broadcasted_iota(I32, s.shape, 0) % chunks
    for j in range(nlev):
        d = 1 << j
        sh = jnp.where(row >= d, pltpu.roll(s, d, axis=0), 0.0)
        sw = pltpu.roll(sh, p, axis=1)
        s = s + a1_ref[0, j:j + 1, :] * sh + a2_ref[0, j:j + 1, :] * sw
    sp = jnp.where(row >= 1, pltpu.roll(s, 1, axis=0), 0.0)
    y = _dot(u, mi_ref[0]) + _dot(sp.astype(BF16), mo_ref[0]) + d_ref[0] * u.astype(F32)
    y_ref[0] = y


def _s5_mixer(ug, mi, ms, mo, a1, a2, dsk, *, chunks):
    g, r, th = ug.shape
    p2 = ms.shape[2]
    nlev = a1.shape[1]
    kernel = functools.partial(_s5_mixer_kernel, nlev=nlev, chunks=chunks, p=p2 // 2)
    return pl.pallas_call(
        kernel,
        grid=(g,),
        in_specs=[pl.BlockSpec((1, r, th), lambda i: (i, 0, 0)),
                  pl.BlockSpec((1, th, th), lambda i: (i, 0, 0)),
                  pl.BlockSpec((1, th, p2), lambda i: (i, 0, 0)),
                  pl.BlockSpec((1, p2, th), lambda i: (i, 0, 0)),
                  pl.BlockSpec((1, nlev, p2), lambda i: (i, 0, 0)),
                  pl.BlockSpec((1, nlev, p2), lambda i: (i, 0, 0)),
                  pl.BlockSpec((1, 1, th), lambda i: (i, 0, 0))],
        out_specs=pl.BlockSpec((1, r, th), lambda i: (i, 0, 0)),
        out_shape=jax.ShapeDtypeStruct((g, r, th), F32),
        compiler_params=_cparams(("parallel",), VMEM_LIMIT),
        name="s5_mixer",
    )(ug, mi, ms, mo, a1, a2, dsk)


def _s5_fused_kernel(u_ref, mi_ref, ms_ref, mo_ref, a1_ref, a2_ref, d_ref, y_ref, tmp_sc, us_sc,
                     *, nlev, chunks, p, t):
    h = SSM_GROUP
    gpb = LANES // h
    n = u_ref.shape[0]
    r = n // t
    nblk = t * h // LANES
    tpb = LANES // h
    rc = tmp_sc.shape[0] // t

    def regroup(c, carry):
        tmp_sc[...] = u_ref[pl.ds(pl.multiple_of(c * rc * t, rc * t), rc * t), :].astype(F32)
        for tau in range(t):
            us_sc[tau, pl.ds(pl.multiple_of(c * rc, rc), rc), :] = tmp_sc[pl.ds(tau, rc, stride=t), :]
        return carry

    lax.fori_loop(0, r // rc, regroup, 0)
    grp = lax.broadcasted_iota(I32, (r, LANES), 1) // h
    row = lax.broadcasted_iota(I32, (r, 2 * p), 0) % chunks

    def group(g, carry):
        blocks = []
        for ob in range(nblk):
            acc = jnp.zeros((r, LANES), F32)
            for q in range(tpb):
                s = us_sc[ob * tpb + q]
                acc = jnp.where(grp == q, pltpu.roll(s, ((q - g) * h) % LANES, axis=1), acc)
            blocks.append(acc)
        uf = jnp.concatenate(blocks, axis=1)
        u = uf.astype(BF16)
        s = _dot(u, ms_ref[g])
        for j in range(nlev):
            d = 1 << j
            sh = jnp.where(row >= d, pltpu.roll(s, d, axis=0), 0.0)
            sw = pltpu.roll(sh, p, axis=1)
            s = s + a1_ref[g, j:j + 1, :] * sh + a2_ref[g, j:j + 1, :] * sw
        sp = jnp.where(row >= 1, pltpu.roll(s, 1, axis=0), 0.0)
        y = _dot(u, mi_ref[g]) + _dot(sp.astype(BF16), mo_ref[g]) + d_ref[g] * u.astype(F32)
        def scatter_lanes(first):
            for ob in range(nblk):
                yb = y[:, ob * LANES:(ob + 1) * LANES]
                for q in range(tpb):
                    tau = ob * tpb + q
                    moved = pltpu.roll(yb, ((g - q) * h) % LANES, axis=1)
                    y_ref[tau] = jnp.where(grp == g, moved, 0.0 if first else y_ref[tau])

        @pl.when(g == 0)
        def _():
            scatter_lanes(True)

        @pl.when(g > 0)
        def _():
            scatter_lanes(False)

        return carry

    lax.fori_loop(0, gpb, group, 0)


def _s5_fused(u, mi, ms, mo, a1, a2, dsk, *, col_blocks, t, chunks):
    n = u.shape[0]
    gpb = LANES // SSM_GROUP
    th = t * SSM_GROUP
    p2 = ms.shape[2]
    nlev = a1.shape[1]
    r = n // t
    kernel = functools.partial(_s5_fused_kernel, nlev=nlev, chunks=chunks, p=p2 // 2, t=t)
    blk3 = lambda i: (i, 0, 0)
    return pl.pallas_call(
        kernel,
        grid=(col_blocks,),
        in_specs=[pl.BlockSpec((n, LANES), lambda i: (0, i)),
                  pl.BlockSpec((gpb, th, th), blk3),
                  pl.BlockSpec((gpb, th, p2), blk3),
                  pl.BlockSpec((gpb, p2, th), blk3),
                  pl.BlockSpec((gpb, nlev, p2), blk3),
                  pl.BlockSpec((gpb, nlev, p2), blk3),
                  pl.BlockSpec((gpb, 1, th), blk3)],
        out_specs=pl.BlockSpec((t, r, LANES), lambda i: (0, 0, i)),
        out_shape=jax.ShapeDtypeStruct((t, r, col_blocks * LANES), F32),
        scratch_shapes=[pltpu.VMEM((min(r, S5_REGROUP_ROWS) * t, LANES), F32), pltpu.VMEM((t, r, LANES), F32)],
        compiler_params=_cparams(("parallel",), VMEM_LIMIT),
        name="s5_mixer",
    )(u, mi, ms, mo, a1, a2, dsk)


def _s5_params(lam_re, lam_im, log_step, b_re, b_im, c_re, c_im, d_skip, t, chunks):
    g, p = lam_re.shape
    h = SSM_GROUP
    lr, li = lam_re.astype(F32), lam_im.astype(F32)
    dt = jnp.exp(log_step.astype(F32))[:, None]
    mag = jnp.exp(lr * dt)
    lb_re, lb_im = mag * jnp.cos(li * dt), mag * jnp.sin(li * dt)
    den = lr * lr + li * li
    nr, ni = lb_re - 1.0, lb_im
    coef_re = (nr * lr + ni * li) / den
    coef_im = (ni * lr - nr * li) / den
    br, bi = b_re.astype(F32), b_im.astype(F32)
    bb_re = coef_re[..., None] * br - coef_im[..., None] * bi
    bb_im = coef_re[..., None] * bi + coef_im[..., None] * br

    def power(k):
        return jnp.exp(lr * dt * k) * jnp.cos(li * dt * k), jnp.exp(lr * dt * k) * jnp.sin(li * dt * k)

    kk = jnp.arange(t + 1, dtype=F32)[:, None, None]
    pw_re, pw_im = power(kk)
    wb_re = pw_re[:t, ..., None] * bb_re - pw_im[:t, ..., None] * bb_im
    wb_im = pw_re[:t, ..., None] * bb_im + pw_im[:t, ..., None] * bb_re
    wk = jnp.concatenate([wb_re, wb_im], axis=2).transpose(1, 0, 3, 2)
    ms = wk[:, ::-1].reshape(g, t * h, 2 * p)
    cre, cim = c_re.astype(F32), c_im.astype(F32)
    cc = jnp.concatenate([cre.transpose(0, 2, 1), -cim.transpose(0, 2, 1)], axis=1)
    taps = _s5_kernel_taps(wk.reshape(g, t * h, 2 * p), cc)
    kt = taps.reshape(g, t, h, h).transpose(0, 2, 1, 3).reshape(g, h, t * h)
    kpad = jnp.concatenate([jnp.zeros_like(kt), kt], axis=2)
    mi = jnp.stack([kpad[:, :, (t - s) * h:(2 * t - s) * h] for s in range(t)], axis=1)
    mi = mi.reshape(g, t * h, t * h)
    cre_t, cim_t = cre.transpose(0, 2, 1)[:, :, None, :], cim.transpose(0, 2, 1)[:, :, None, :]
    are = pw_re[1:].transpose(1, 2, 0)[..., None]
    aim = pw_im[1:].transpose(1, 2, 0)[..., None]
    mo_re = cre_t * are - cim_t * aim
    mo_im = -(cre_t * aim + cim_t * are)
    mo = jnp.concatenate([mo_re, mo_im], axis=1).reshape(g, 2 * p, t * h)
    nlev = max(1, (chunks - 1).bit_length())
    lv = (float(t) * (2.0 ** jnp.arange(nlev, dtype=F32)))[None, :, None]
    ar, ai = power_levels(lr, li, dt, lv)
    a1 = jnp.concatenate([ar, ar], axis=2)
    a2 = jnp.concatenate([-ai, ai], axis=2)
    dsk = jnp.tile(d_skip.astype(F32).reshape(g, 1, h), (1, 1, t))
    return mi.astype(BF16), ms.astype(BF16), mo.astype(BF16), a1, a2, dsk


def power_levels(lr, li, dt, lv):
    e = (lr * dt)[:, None, :] * lv
    w = (li * dt)[:, None, :] * lv
    return jnp.exp(e) * jnp.cos(w), jnp.exp(e) * jnp.sin(w)


def _mix_out_kernel(y_ref, att_ref, x_ref, wglu_ref, bglu_ref, gs_ref, ga_ref, woa_ref, wos_ref, o_ref, y_sc):
    t = y_ref.shape[0]
    for rl in range(y_ref.shape[1]):
        y_sc[rl * t:(rl + 1) * t, :] = y_ref[:, rl, :]
    y = jax.nn.gelu(y_sc[...])
    z = _dot(y.astype(BF16), wglu_ref[...]) + bglu_ref[...]
    ssm = y * jax.nn.sigmoid(z)
    na = _rms(att_ref[...], ga_ref[...]).astype(BF16)
    ns = _rms(ssm, gs_ref[...]).astype(BF16)
    o_ref[...] = x_ref[...] + _dot(na, woa_ref[...]) + _dot(ns, wos_ref[...])


def _mix_out(ypre, att, x, w_glu, b_glu, g_ssm, g_att, wo_a, wo_s, tm):
    n, d = x.shape
    t, _, sw = ypre.shape
    aw = att.shape[1]
    tm = min(tm, n)
    row = lambda i: (i, 0)
    fix = lambda i: (0, 0)
    once = dict(pipeline_mode=pl.Buffered(1))
    return pl.pallas_call(
        _mix_out_kernel,
        grid=(n // tm,),
        in_specs=[pl.BlockSpec((t, tm // t, sw), lambda i: (0, i, 0)),
                  pl.BlockSpec((tm, aw), row), pl.BlockSpec((tm, d), row),
                  pl.BlockSpec((sw, sw), fix, **once), pl.BlockSpec((1, sw), fix),
                  pl.BlockSpec((1, sw), fix), pl.BlockSpec((1, aw), fix),
                  pl.BlockSpec((aw, d), fix, **once), pl.BlockSpec((sw, d), fix, **once)],
        out_specs=pl.BlockSpec((tm, d), row),
        out_shape=jax.ShapeDtypeStruct((n, d), F32),
        scratch_shapes=[pltpu.VMEM((tm, sw), F32)],
        compiler_params=_cparams(("parallel",), VMEM_LIMIT),
        name="mix_out",
    )(ypre, att, x, w_glu, b_glu.reshape(1, sw), g_ssm.reshape(1, sw), g_att.reshape(1, aw), wo_a, wo_s)


def _cross_kernel(h_ref, g_ref, wq_ref, kv_ref, wo_ref, o_ref, *, heads):
    h = h_ref[...]
    d = h.shape[1]
    dh = d // heads
    q = _dot(_rms(h, g_ref[...]).astype(BF16), wq_ref[...]).astype(BF16)
    out = h
    for hd in range(heads):
        lo = hd * dh
        s = _dot_t(q[:, lo:lo + dh], kv_ref[:, lo:lo + dh]) * dh ** -0.5
        e = jnp.exp(s - jnp.max(s, axis=1, keepdims=True))
        p = e / jnp.sum(e, axis=1, keepdims=True)
        oh = _dot(p.astype(BF16), kv_ref[:, d + lo:d + lo + dh])
        out = out + _dot(oh.astype(BF16), wo_ref[lo:lo + dh, :])
    o_ref[...] = out


def _cross_attention(h, g, wq, kv, wo, *, batch, mem_len, tm):
    n, d = h.shape
    tm = min(tm, n // batch)
    tiles = n // batch // tm
    fix = lambda i: (0, 0)
    once = dict(pipeline_mode=pl.Buffered(1))
    return pl.pallas_call(
        functools.partial(_cross_kernel, heads=CROSS_HEADS),
        grid=(n // tm,),
        in_specs=[pl.BlockSpec((tm, d), lambda i: (i, 0)),
                  pl.BlockSpec((1, d), fix),
                  pl.BlockSpec((d, d), fix, **once),
                  pl.BlockSpec((mem_len, 2 * d), lambda i: (i // tiles, 0)),
                  pl.BlockSpec((d, d), fix, **once)],
        out_specs=pl.BlockSpec((tm, d), lambda i: (i, 0)),
        out_shape=jax.ShapeDtypeStruct((n, d), F32),
        compiler_params=_cparams(("parallel",), VMEM_LIMIT),
        name="cross_attention",
    )(h, g.reshape(1, d), wq, kv, wo)


def _router_kernel(h_ref, g_ref, wr_ref, br_ref, xm_ref, idx_ref, gate_ref, rank_ref, cnt_ref, carry_sc):
    @pl.when(pl.program_id(0) == 0)
    def _():
        carry_sc[...] = jnp.zeros(carry_sc.shape, F32)

    hm = _rms(h_ref[...], g_ref[...])
    xm_ref[...] = hm
    logits = _dot(hm, wr_ref[...], HIGHEST) + br_ref[...]
    tm = logits.shape[0]
    lane = lax.broadcasted_iota(I32, logits.shape, 1).astype(F32)
    work = logits
    vals, idxs = [], []
    for _ in range(TOP_K):
        m = jnp.max(work, axis=1, keepdims=True)
        am = jnp.min(jnp.where(work == m, lane, float(LANES)), axis=1, keepdims=True)
        vals.append(m)
        idxs.append(am)
        work = jnp.where(lane == am, NEG_BIG, work)
    es = [jnp.exp(v - vals[0]) for v in vals]
    den = es[0] + es[1] + es[2] + es[3]
    sel = jnp.zeros(logits.shape, F32)
    for am in idxs:
        sel = sel + (lane == am).astype(F32)
    r = lax.broadcasted_iota(I32, (tm, tm), 0)
    c = lax.broadcasted_iota(I32, (tm, tm), 1)
    before = (c < r).astype(BF16)
    rank_all = carry_sc[0:1, :] + _dot(before, sel.astype(BF16))
    idx_o = jnp.zeros(logits.shape, F32)
    gate_o = jnp.zeros(logits.shape, F32)
    rank_o = jnp.zeros(logits.shape, F32)
    for j in range(TOP_K):
        slot = lane == float(j)
        rk = jnp.sum(jnp.where(lane == idxs[j], rank_all, 0.0), axis=1, keepdims=True)
        idx_o = jnp.where(slot, idxs[j], idx_o)
        gate_o = jnp.where(slot, es[j] / den, gate_o)
        rank_o = jnp.where(slot, rk, rank_o)
    idx_ref[...] = idx_o.astype(I32)
    gate_ref[...] = gate_o
    rank_ref[...] = rank_o.astype(I32)
    carry_sc[...] = carry_sc[...] + jnp.sum(sel, axis=0, keepdims=True)
    cnt_ref[...] = carry_sc[...].astype(I32)


def _router(h, g, wr, br, tm):
    n, d = h.shape
    tm = min(tm, n)
    row = lambda i: (i, 0)
    fix = lambda i: (0, 0)
    return pl.pallas_call(
        _router_kernel,
        grid=(n // tm,),
        in_specs=[pl.BlockSpec((tm, d), row), pl.BlockSpec((1, d), fix),
                  pl.BlockSpec((d, LANES), fix), pl.BlockSpec((1, LANES), fix)],
        out_specs=[pl.BlockSpec((tm, d), row), pl.BlockSpec((tm, LANES), row),
                   pl.BlockSpec((tm, LANES), row), pl.BlockSpec((tm, LANES), row),
                   pl.BlockSpec((8, LANES), fix)],
        out_shape=[jax.ShapeDtypeStruct((n, d), F32), jax.ShapeDtypeStruct((n, LANES), I32),
                   jax.ShapeDtypeStruct((n, LANES), F32), jax.ShapeDtypeStruct((n, LANES), I32),
                   jax.ShapeDtypeStruct((8, LANES), I32)],
        scratch_shapes=[pltpu.VMEM((8, LANES), F32)],
        compiler_params=_cparams(("arbitrary",), VMEM_LIMIT),
        name="router",
    )(h, g.reshape(1, d), wr, br)


def _row_copy(src_hbm, row, dst, slot, sem):
    return pltpu.make_async_copy(src_hbm.at[pl.ds(row, 1), :], dst.at[pl.ds(slot, 1), :], sem)


def _slab_copy(src_hbm, dst, slot, rows, sem):
    return pltpu.make_async_copy(src_hbm.at[pl.ds(0, rows), :], dst.at[pl.ds(slot, rows), :], sem)


def _expert_kernel(sbe_ref, sbn_ref, nused_ref, tok_ref, tokn_ref, xm_hbm, wg_ref, wl_ref, wd_ref, bg_ref,
                   bl_ref, bd_ref, o_ref, xg_ref, xb_ref, wgb_ref, wlb_ref, wdb_ref, sem, *, nf):
    i = pl.program_id(0)
    j = pl.program_id(1)
    n_i = pl.num_programs(0)
    sb_rows = xg_ref.shape[0]
    nrows = sbn_ref[i]
    nsub = (nrows + MOE_SUB - 1) // MOE_SUB

    def wait_rows():
        for s in range(MOE_SUBS):
            _slab_copy(xm_hbm, xg_ref, s * MOE_SUB, MOE_SUB, sem).wait()

    @pl.when(jnp.logical_and(i == 0, j == 0))
    def _():
        def issue(r8, c):
            for u in range(ISSUE_UNROLL):
                r = r8 * ISSUE_UNROLL + u
                _row_copy(xm_hbm, tok_ref[0, 0, r], xg_ref, r, sem).start()
            return c

        lax.fori_loop(0, sb_rows // ISSUE_UNROLL, issue, 0)

    @pl.when(j == 0)
    def _():
        for s in range(MOE_SUBS):
            rows = pl.ds(s * MOE_SUB, MOE_SUB)

            @pl.when(s < nsub)
            def _():
                o_ref[rows, :] = jnp.broadcast_to(bd_ref[...], (MOE_SUB, o_ref.shape[1]))

            @pl.when(s >= nsub)
            def _():
                o_ref[rows, :] = jnp.zeros((MOE_SUB, o_ref.shape[1]), F32)

        wait_rows()
        for s in range(MOE_SUBS):
            rows = pl.ds(s * MOE_SUB, MOE_SUB)

            @pl.when(s < nsub)
            def _():
                xb_ref[rows, :] = xg_ref[rows, :].astype(BF16)

    def prefetch_next():
        step_rows = sb_rows // nf
        for u in range(step_rows):
            r = j * step_rows + u
            _row_copy(xm_hbm, tokn_ref[0, 0, r], xg_ref, r, sem).start()

    def ffn(rows):
        x = xb_ref[rows, :]
        g = jnp.minimum(_dot(x, wgb_ref[...]) + bg_ref[...], SWIGLU_LIMIT)
        lin = jnp.clip(_dot(x, wlb_ref[...]) + bl_ref[...], -SWIGLU_LIMIT, SWIGLU_LIMIT)
        act = (g * jax.nn.sigmoid(SWIGLU_ALPHA * g) * (lin + 1.0)).astype(BF16)
        d = o_ref.shape[1]
        for c in range(0, d, MOE_NCHUNK):
            cols = pl.ds(c, min(MOE_NCHUNK, d - c))
            o_ref[rows, cols] = o_ref[rows, cols] + _dot(act, wdb_ref[:, cols])

    @pl.when(nrows > 0)
    def _():
        wgb_ref[...] = wg_ref[...].astype(BF16)
        wlb_ref[...] = wl_ref[...].astype(BF16)
        wdb_ref[...] = wd_ref[...].astype(BF16)

    @pl.when(nsub == MOE_SUBS)
    def _():
        prefetch_next()
        ffn(pl.ds(0, MOE_SUBS * MOE_SUB))

    @pl.when(nsub < MOE_SUBS)
    def _():
        prefetch_next()

    for s in range(MOE_SUBS - 1):
        @pl.when(jnp.logical_and(s < nsub, nsub < MOE_SUBS))
        def _():
            ffn(pl.ds(s * MOE_SUB, MOE_SUB))

    @pl.when(jnp.logical_and(i == n_i - 1, j == nf - 1))
    def _():
        wait_rows()


def _experts(xm, row_tok, sb_e, sb_n, n_used, w_gu, b_gu, w_dn, b_dn):
    n_sb = sb_e.shape[0]
    sb_rows = MOE_SUBS * MOE_SUB
    e, d, f2 = w_gu.shape
    f = f2 // 2
    tf = min(MOE_FTILE, f)
    nf = f // tf

    def jj(i, j, nused):
        return jnp.where(i < nused[0], j, nf - 1)

    grid_spec = pltpu.PrefetchScalarGridSpec(
        num_scalar_prefetch=3,
        grid=(n_sb, nf),
        in_specs=[
            pl.BlockSpec((1, 1, sb_rows), lambda i, j, sbe, sbn, nu: (i, 0, 0), memory_space=pltpu.SMEM),
            pl.BlockSpec((1, 1, sb_rows), lambda i, j, sbe, sbn, nu: (jnp.minimum(i + 1, n_sb - 1), 0, 0),
                         memory_space=pltpu.SMEM),
            pl.BlockSpec(memory_space=pl.ANY),
            pl.BlockSpec((None, d, tf), lambda i, j, sbe, sbn, nu: (sbe[i], 0, jj(i, j, nu))),
            pl.BlockSpec((None, d, tf), lambda i, j, sbe, sbn, nu: (sbe[i], 0, nf + jj(i, j, nu))),
            pl.BlockSpec((None, tf, d), lambda i, j, sbe, sbn, nu: (sbe[i], jj(i, j, nu), 0)),
            pl.BlockSpec((None, 1, tf), lambda i, j, sbe, sbn, nu: (sbe[i], 0, jj(i, j, nu))),
            pl.BlockSpec((None, 1, tf), lambda i, j, sbe, sbn, nu: (sbe[i], 0, nf + jj(i, j, nu))),
            pl.BlockSpec((None, 1, d), lambda i, j, sbe, sbn, nu: (sbe[i], 0, 0)),
        ],
        out_specs=pl.BlockSpec((sb_rows, d), lambda i, j, sbe, sbn, nu: (jnp.where(i < nu[0], i, n_sb), 0)),
        scratch_shapes=[pltpu.VMEM((sb_rows, d), F32), pltpu.VMEM((sb_rows, d), BF16),
                        pltpu.VMEM((d, tf), BF16), pltpu.VMEM((d, tf), BF16), pltpu.VMEM((tf, d), BF16),
                        pltpu.SemaphoreType.DMA],
    )
    return pl.pallas_call(
        functools.partial(_expert_kernel, nf=nf),
        grid_spec=grid_spec,
        out_shape=jax.ShapeDtypeStruct(((n_sb + 1) * sb_rows, d), F32),
        compiler_params=_cparams(("arbitrary", "arbitrary"), VMEM_LIMIT),
        name="experts",
    )(sb_e, sb_n, n_used, row_tok.reshape(n_sb, 1, sb_rows), row_tok.reshape(n_sb, 1, sb_rows), xm, w_gu, w_gu, w_dn,
      b_gu.reshape(e, 1, f2), b_gu.reshape(e, 1, f2), b_dn.reshape(e, 1, d))


def _expert2_kernel(sbe_ref, sbn_ref, sbf_ref, nused_ref, tok_ref, tokn_ref, xm_hbm, wgu_hbm, wdn_hbm, bgu_ref, bdn_ref,
                    o_ref, xg_ref, xb_ref, wg_res, wl_res, wd_res, stg_g, stg_l, stg_d, gsem, wsem,
                    *, nf, tf, n_sb):
    i = pl.program_id(0)
    slot = i % 2
    e = sbe_ref[i]
    nsub = (sbn_ref[i] + MOE_SUB - 1) // MOE_SUB
    first = sbf_ref[i]
    sb_rows, d = xb_ref.shape
    subs = sb_rows // MOE_SUB
    step_rows = sb_rows // nf

    def weight_copies(expert, j, s):
        lo = pl.multiple_of(j * tf, tf)
        hi = pl.multiple_of(nf * tf + j * tf, tf)
        return (pltpu.make_async_copy(wgu_hbm.at[expert, :, pl.ds(lo, tf)], stg_g.at[s], wsem.at[0, s]),
                pltpu.make_async_copy(wgu_hbm.at[expert, :, pl.ds(hi, tf)], stg_l.at[s], wsem.at[1, s]),
                pltpu.make_async_copy(wdn_hbm.at[expert, pl.ds(lo, tf), :], stg_d.at[s], wsem.at[2, s]))

    def start_first_tiles(expert):
        for s in range(2):
            for cp in weight_copies(expert, s, s):
                cp.start()

    nxt = jnp.minimum(i + 1, n_sb - 1)
    nsub_next = jnp.where(i + 1 < n_sb, (sbn_ref[nxt] + MOE_SUB - 1) // MOE_SUB, 0)
    next_is_first = jnp.logical_and(i + 1 < n_sb, sbf_ref[nxt] == 1)

    @pl.when(i == 0)
    def _():
        start_first_tiles(e)

        def issue(r8, c):
            for u in range(ISSUE_UNROLL):
                r = r8 * ISSUE_UNROLL + u
                _row_copy(xm_hbm, tok_ref[0, 0, r], xg_ref.at[0], r, gsem).start()
            return c

        lax.fori_loop(0, nsub * (MOE_SUB // ISSUE_UNROLL), issue, 0)

    for s in range(subs):
        @pl.when(s < nsub)
        def _():
            _slab_copy(xm_hbm, xg_ref.at[slot], s * MOE_SUB, MOE_SUB, gsem).wait()

    for s in range(subs):
        @pl.when(s < nsub)
        def _():
            rows = pl.ds(s * MOE_SUB, MOE_SUB)
            xb_ref[rows, :] = xg_ref[slot, rows, :].astype(BF16)

    o_ref[...] = jnp.broadcast_to(bdn_ref[...], o_ref.shape)

    @pl.when(jnp.logical_and(next_is_first, first == 0))
    def _():
        start_first_tiles(sbe_ref[nxt])

    def gather_next(j):
        @pl.when((j * step_rows) // MOE_SUB < nsub_next)
        def _():
            for u in range(step_rows):
                r = j * step_rows + u
                _row_copy(xm_hbm, tokn_ref[0, 0, r], xg_ref.at[1 - slot], r, gsem).start()

    def ffn(j, rows):
        x = xb_ref[rows, :]
        g = jnp.minimum(_dot(x, wg_res[j]) + bgu_ref[j], SWIGLU_LIMIT)
        lin = jnp.clip(_dot(x, wl_res[j]) + bgu_ref[nf + j], -SWIGLU_LIMIT, SWIGLU_LIMIT)
        act = (g * jax.nn.sigmoid(SWIGLU_ALPHA * g) * (lin + 1.0)).astype(BF16)
        for c in range(0, d, MOE_NCHUNK):
            cols = pl.ds(c, min(MOE_NCHUNK, d - c))
            o_ref[rows, cols] = o_ref[rows, cols] + _dot(act, wd_res[j, :, cols])

    def tile(j, carry):
        s = j % 2

        @pl.when(first == 1)
        def _():
            for cp in weight_copies(e, j, s):
                cp.wait()
            wg_res[j] = stg_g[s].astype(BF16)
            wl_res[j] = stg_l[s].astype(BF16)
            wd_res[j] = stg_d[s].astype(BF16)

            @pl.when(j + 2 < nf)
            def _():
                for cp in weight_copies(e, j + 2, s):
                    cp.start()

        gather_next(j)
        for k in range(1, subs + 1):
            @pl.when(nsub == k)
            def _():
                ffn(j, pl.ds(0, k * MOE_SUB))

        return carry

    lax.fori_loop(0, nf, tile, 0)

    @pl.when(jnp.logical_and(next_is_first, first == 1))
    def _():
        start_first_tiles(sbe_ref[nxt])


def _experts2(xm, row_tok, sb_e, sb_n, sb_f, n_used, w_gu, b_gu, w_dn, b_dn):
    n_sb = sb_e.shape[0]
    sb_rows = MOE2_SB
    e, d, f2 = w_gu.shape
    f = f2 // 2
    tf = min(MOE_FTILE, f)
    nf = f // tf
    assert nf >= 2 and sb_rows % nf == 0 and MOE_SUB % (sb_rows // nf) == 0
    tok3 = row_tok.reshape(n_sb, 1, sb_rows)
    grid_spec = pltpu.PrefetchScalarGridSpec(
        num_scalar_prefetch=4,
        grid=(n_sb,),
        in_specs=[
            pl.BlockSpec((1, 1, sb_rows), lambda i, sbe, sbn, sbf, nu: (i, 0, 0), memory_space=pltpu.SMEM),
            pl.BlockSpec((1, 1, sb_rows), lambda i, sbe, sbn, sbf, nu: (jnp.minimum(i + 1, n_sb - 1), 0, 0),
                         memory_space=pltpu.SMEM),
            pl.BlockSpec(memory_space=pl.ANY),
            pl.BlockSpec(memory_space=pl.ANY),
            pl.BlockSpec(memory_space=pl.ANY),
            pl.BlockSpec((None, 2 * nf, 1, tf), lambda i, sbe, sbn, sbf, nu: (sbe[i], 0, 0, 0)),
            pl.BlockSpec((None, 1, d), lambda i, sbe, sbn, sbf, nu: (sbe[i], 0, 0)),
        ],
        out_specs=pl.BlockSpec((sb_rows, d), lambda i, sbe, sbn, sbf, nu: (jnp.where(i < nu[0], i, n_sb), 0)),
        scratch_shapes=[pltpu.VMEM((2, sb_rows, d), F32), pltpu.VMEM((sb_rows, d), BF16),
                        pltpu.VMEM((nf, d, tf), BF16), pltpu.VMEM((nf, d, tf), BF16), pltpu.VMEM((nf, tf, d), BF16),
                        pltpu.VMEM((2, d, tf), F32), pltpu.VMEM((2, d, tf), F32), pltpu.VMEM((2, tf, d), F32),
                        pltpu.SemaphoreType.DMA, pltpu.SemaphoreType.DMA((3, 2))],
    )
    return pl.pallas_call(
        functools.partial(_expert2_kernel, nf=nf, tf=tf, n_sb=n_sb),
        grid_spec=grid_spec,
        out_shape=jax.ShapeDtypeStruct(((n_sb + 1) * sb_rows, d), F32),
        compiler_params=_cparams(("arbitrary",), MOE2_VMEM_LIMIT),
        name="experts",
    )(sb_e, sb_n, sb_f, n_used, tok3, tok3, xm, w_gu, w_dn, b_gu.reshape(e, 2 * nf, 1, tf), b_dn.reshape(e, 1, d))


def _combine_kernel(dest_ref, gate_ref, h_ref, g_ref, y_hbm, o_ref, buf_ref, sem):
    tm = h_ref.shape[0]

    tok_unroll = ISSUE_UNROLL // TOP_K

    def issue(t2, c):
        for u in range(tok_unroll):
            t = t2 * tok_unroll + u
            for k in range(TOP_K):
                _row_copy(y_hbm, dest_ref[0, 0, t * TOP_K + k], buf_ref.at[k], t, sem).start()
        return c

    lax.fori_loop(0, tm // tok_unroll, issue, 0)
    acc = h_ref[...]
    gates = gate_ref[...]
    for k in range(TOP_K):
        _slab_copy(y_hbm, buf_ref.at[k], 0, tm, sem).wait()
    for k in range(TOP_K):
        acc = acc + gates[:, k:k + 1] * buf_ref[k]
    o_ref[...] = _rms(acc, g_ref[...])


def _combine(dest, gates, h, g, yrows, tm):
    n, d = h.shape
    tm = min(tm, n)
    return pl.pallas_call(
        _combine_kernel,
        grid=(n // tm,),
        in_specs=[pl.BlockSpec((1, 1, tm * TOP_K), lambda i: (i, 0, 0), memory_space=pltpu.SMEM),
                  pl.BlockSpec((tm, LANES), lambda i: (i, 0)),
                  pl.BlockSpec((tm, d), lambda i: (i, 0)),
                  pl.BlockSpec((1, d), lambda i: (0, 0)),
                  pl.BlockSpec(memory_space=pl.ANY)],
        out_specs=pl.BlockSpec((tm, d), lambda i: (i, 0)),
        out_shape=jax.ShapeDtypeStruct((n, d), F32),
        scratch_shapes=[pltpu.VMEM((TOP_K, tm, d), F32), pltpu.SemaphoreType.DMA],
        compiler_params=_cparams(("arbitrary",), VMEM_LIMIT),
        name="combine",
    )(dest.reshape(n // tm, 1, tm * TOP_K), gates, h, g.reshape(1, d), yrows)


def _layer(h, mem2d, p, *, batch, length):
    n, d = h.shape
    g_cnt, p_state = p["lam_re"].shape
    sw = g_cnt * SSM_GROUP
    aw = d - sw
    heads = aw // ATTN_HEAD_DIM
    assert p["w_in"].shape[1] == sw + 3 * aw + heads and sw % LANES == 0 and 2 * p_state == LANES

    w_in = p["w_in"]
    cols = sw + 3 * aw
    q_scale = ATTN_HEAD_DIM ** -0.5 * LOG2E
    cs = jnp.concatenate([jnp.ones((sw,), F32), jnp.full((aw,), q_scale, F32), jnp.ones((2 * aw,), F32)])
    w_f = jnp.pad(w_in[:, cols:], ((0, 0), (0, LANES - heads)))
    proj, flog = _in_proj(h, p["g_mix"], w_in[:, :cols].astype(BF16), cs, w_f, 1024, 1024)

    f_rows = flog[:, :heads].reshape(batch, length, heads).transpose(0, 2, 1).reshape(batch * heads, length)
    bias_rows = jnp.tile(p["b_f"].astype(F32), batch).reshape(batch * heads, 1)
    cum_rows = _fgate_cumsum(f_rows, bias_rows)
    cum_col = cum_rows.reshape(batch, heads, length).transpose(0, 2, 1).reshape(n, heads)

    blk = lambda c: c // ATTN_HEAD_DIM
    att = _fox_attention(proj, cum_col, cum_rows.reshape(batch * heads, 1, length), batch=batch, length=length,
                         heads=heads, q_col=blk(sw), k_col=blk(sw + aw), v_col=blk(sw + 2 * aw))

    t = min(S5_CHUNK, length)
    chunks = length // t
    mi, ms, mo, a1, a2, dsk = _s5_params(p["lam_re"], p["lam_im"], p["log_step"], p["b_re"], p["b_im"],
                                         p["c_re"], p["c_im"], p["d_skip"], t, chunks)
    ypre = _s5_fused(proj, mi, ms, mo, a1, a2, dsk, col_blocks=sw // LANES, t=t, chunks=chunks)

    w_out = p["w_out"].astype(BF16)
    h = _mix_out(ypre, att, h, p["w_glu"].astype(BF16), p["b_glu"], p["g_ssm_out"], p["g_attn_out"],
                 w_out[:aw], w_out[aw:], 256)

    mem_len = mem2d.shape[0] // batch
    kv = _norm_matmul(mem2d, p["g_mem"], p["w_ckv"].astype(BF16), jnp.ones((2 * d,), F32), BF16, 512, 512)
    h = _cross_attention(h, p["g_cross"], p["w_cq"].astype(BF16), kv, p["w_co"].astype(BF16),
                         batch=batch, mem_len=mem_len, tm=512)

    dest, gates, yrows = _moe(h, p)
    return dest, gates, h, yrows


def _moe(h, p):
    n = h.shape[0]
    n_exp = p["w_router"].shape[1]
    wr = jnp.pad(p["w_router"].astype(F32), ((0, 0), (0, LANES - n_exp)))
    br = jnp.pad(p["b_router"].astype(F32), (0, LANES - n_exp), constant_values=NEG_BIG).reshape(1, LANES)
    xm, idx, gates, rank, cnt = _router(h, p["g_moe"], wr, br, 256)

    sb_rows = MOE2_SB
    n_sb = (n * TOP_K) // sb_rows + n_exp
    counts = cnt[0, :n_exp]
    padded = (counts + sb_rows - 1) // sb_rows * sb_rows
    pend = jnp.cumsum(padded)
    pstart = pend - padded
    idx_k, rank_k = idx[:, :TOP_K], rank[:, :TOP_K]
    onehot = idx_k[..., None] == jnp.arange(n_exp, dtype=I32)
    dest = rank_k + jnp.sum(jnp.where(onehot, pstart, 0), axis=-1)
    tok = jnp.broadcast_to(jnp.arange(n, dtype=I32)[:, None], (n, TOP_K))
    row_tok = jnp.zeros((n_sb * sb_rows,), I32).at[dest.reshape(-1)].set(tok.reshape(-1), unique_indices=True)
    n_used = (pend[-1] // sb_rows).astype(I32)
    sb_start = jnp.arange(n_sb, dtype=I32) * sb_rows
    sb_e = jnp.clip(jnp.sum(sb_start[:, None] >= pend[None, :], axis=1), 0, n_exp - 1).astype(I32)
    used = jnp.arange(n_sb, dtype=I32) < n_used
    sb_n = jnp.where(used, jnp.clip(counts[sb_e] - (sb_start - pstart[sb_e]), 0, sb_rows), 0).astype(I32)
    sb_f = jnp.logical_and(used, sb_start == pstart[sb_e]).astype(I32)
    sb_e = jnp.where(used, sb_e, sb_e[jnp.maximum(n_used - 1, 0)])

    yrows = _experts2(xm, row_tok, sb_e, sb_n, sb_f, n_used.reshape(1), p["w_gu"], p["b_gu"], p["w_dn"], p["b_dn"])
    return dest, gates, yrows


def kernel(x, mem, g_mix, w_in, b_f, lam_re, lam_im, log_step, b_re, b_im, c_re, c_im, d_skip, w_glu, b_glu, g_attn_out, g_ssm_out, w_out, g_cross, g_mem, w_cq, w_ckv, w_co, g_moe, w_router, b_router, w_gu, b_gu, w_dn, b_dn, g_final):
    batch, length, d = x.shape
    depth = g_mix.shape[0]
    assert depth == 1, "the combine kernel applies the final norm, so exactly one layer is supported"
    names = ("g_mix", "w_in", "b_f", "lam_re", "lam_im", "log_step", "b_re", "b_im", "c_re", "c_im", "d_skip",
             "w_glu", "b_glu", "g_attn_out", "g_ssm_out", "w_out", "g_cross", "g_mem", "w_cq", "w_ckv", "w_co",
             "g_moe", "w_router", "b_router", "w_gu", "b_gu", "w_dn", "b_dn")
    vals = (g_mix, w_in, b_f, lam_re, lam_im, log_step, b_re, b_im, c_re, c_im, d_skip, w_glu, b_glu,
            g_attn_out, g_ssm_out, w_out, g_cross, g_mem, w_cq, w_ckv, w_co, g_moe, w_router, b_router,
            w_gu, b_gu, w_dn, b_dn)
    params = {k: v[0] for k, v in zip(names, vals)}
    h = x.reshape(batch * length, d)
    mem2d = mem.reshape(-1, d)
    dest, gates, h, yrows = _layer(h, mem2d, params, batch=batch, length=length)
    out = _combine(dest, gates, h, g_final, yrows, 256)
    return out.reshape(batch, length, d)
```

```python
import functools

import jax
import jax.numpy as jnp
from jax import lax
from jax.experimental import pallas as pl
from jax.experimental.pallas import tpu as pltpu

F32 = jnp.float32
BF16 = jnp.bfloat16
I32 = jnp.int32

RMS_EPS = 1e-5
SSM_GROUP = 16
ATTN_HEAD_DIM = 128
CROSS_HEADS = 4
TOP_K = 4
SWIGLU_ALPHA = 1.702
SWIGLU_LIMIT = 7.0

LANES = 128
LOG2E = 1.4426950408889634
ATT_BLOCK = 2048
ATT_SQ = 128
ATT_SK = 256
S5_CHUNK = 32
S5_REGROUP_ROWS = 64
MOE_SB = 512
MOE_SUB = 256
MOE_FTILE = 256
MOE_NCHUNK = 512
ISSUE_UNROLL = 8
NEG_BIG = -1e30
VMEM_LIMIT = 56 * 1024 * 1024
MOE_VMEM_LIMIT = 60 * 1024 * 1024

HIGHEST = lax.Precision.HIGHEST


def _cparams(sem, vmem=None):
    return pltpu.CompilerParams(dimension_semantics=sem, vmem_limit_bytes=vmem)


def _rms(x, g):
    return x * lax.rsqrt(jnp.mean(x * x, axis=-1, keepdims=True) + RMS_EPS) * g


def _dot(a, b, precision=None):
    return jnp.dot(a, b, preferred_element_type=F32, precision=precision)


def _dot_t(a, b):
    return lax.dot_general(a, b, (((1,), (1,)), ((), ())), preferred_element_type=F32)


def _norm_matmul_kernel(x_ref, g_ref, w_ref, o_ref, xn_ref):
    @pl.when(pl.program_id(1) == 0)
    def _():
        xn_ref[...] = _rms(x_ref[...], g_ref[...]).astype(xn_ref.dtype)

    o_ref[...] = _dot(xn_ref[...], w_ref[...]).astype(o_ref.dtype)


def _norm_matmul(x, g, w, out_dtype, tm, tn):
    n, k = x.shape
    m = w.shape[1]
    tm, tn = min(tm, n), min(tn, m)
    return pl.pallas_call(
        _norm_matmul_kernel,
        grid=(n // tm, m // tn),
        in_specs=[pl.BlockSpec((tm, k), lambda i, j: (i, 0)),
                  pl.BlockSpec((1, k), lambda i, j: (0, 0)),
                  pl.BlockSpec((k, tn), lambda i, j: (0, j))],
        out_specs=pl.BlockSpec((tm, tn), lambda i, j: (i, j)),
        out_shape=jax.ShapeDtypeStruct((n, m), out_dtype),
        scratch_shapes=[pltpu.VMEM((tm, k), w.dtype)],
        compiler_params=_cparams(("parallel", "arbitrary"), VMEM_LIMIT),
        name="norm_matmul",
    )(x, g.reshape(1, k), w)


def _in_proj_kernel(x_ref, g_ref, w_ref, cs_ref, wf_ref, o_ref, f_ref, xn_ref):
    @pl.when(pl.program_id(1) == 0)
    def _():
        xn = _rms(x_ref[...], g_ref[...])
        xn_ref[...] = xn.astype(xn_ref.dtype)
        f_ref[...] = _dot(xn, wf_ref[...], HIGHEST)

    o_ref[...] = (_dot(xn_ref[...], w_ref[...]) * cs_ref[...]).astype(o_ref.dtype)


def _in_proj(x, g, w, col_scale, w_f, tm, tn):
    n, k = x.shape
    m = w.shape[1]
    tm, tn = min(tm, n), min(tn, m)
    return pl.pallas_call(
        _in_proj_kernel,
        grid=(n // tm, m // tn),
        in_specs=[pl.BlockSpec((tm, k), lambda i, j: (i, 0)),
                  pl.BlockSpec((1, k), lambda i, j: (0, 0)),
                  pl.BlockSpec((k, tn), lambda i, j: (0, j)),
                  pl.BlockSpec((1, tn), lambda i, j: (0, j)),
                  pl.BlockSpec((k, LANES), lambda i, j: (0, 0))],
        out_specs=[pl.BlockSpec((tm, tn), lambda i, j: (i, j)),
                   pl.BlockSpec((tm, LANES), lambda i, j: (i, 0))],
        out_shape=[jax.ShapeDtypeStruct((n, m), BF16), jax.ShapeDtypeStruct((n, LANES), F32)],
        scratch_shapes=[pltpu.VMEM((tm, k), BF16)],
        compiler_params=_cparams(("parallel", "arbitrary"), VMEM_LIMIT),
        name="in_proj",
    )(x, g.reshape(1, k), w, col_scale.reshape(1, m), w_f)


def _fgate_cumsum_kernel(f_ref, b_ref, o_ref):
    z = f_ref[...] + b_ref[...]
    x = jnp.minimum(z, 0.0) - jnp.log(1.0 + jnp.exp(-jnp.abs(z)))
    lane = lax.broadcasted_iota(I32, x.shape, 1)
    d = 1
    while d < x.shape[1]:
        x = x + jnp.where(lane >= d, pltpu.roll(x, d, axis=1), 0.0)
        d *= 2
    o_ref[...] = x * LOG2E


def _fgate_cumsum(f_rows, bias_rows):
    return pl.pallas_call(
        _fgate_cumsum_kernel,
        out_shape=jax.ShapeDtypeStruct(f_rows.shape, F32),
        name="fgate_cumsum",
    )(f_rows, bias_rows)


def _fox_kernel(qi_ref, ki_ref, q_ref, k_ref, v_ref, cq_ref, ck_ref, o_ref, m_sc, l_sc, acc_sc, cq_sc,
                *, blk, sq, sk):
    h = pl.program_id(1)
    t = pl.program_id(2)
    qi = qi_ref[t]
    ki = ki_ref[t]
    rep = sk // LANES

    @pl.when(ki == 0)
    def _():
        m_sc[...] = jnp.full(m_sc.shape, -jnp.inf, F32)
        l_sc[...] = jnp.zeros(l_sc.shape, F32)
        acc_sc[...] = jnp.zeros(acc_sc.shape, F32)
        cq = cq_ref[...]
        lane = lax.broadcasted_iota(I32, cq.shape, 1)
        col = jnp.sum(jnp.where(lane == h, cq, 0.0), axis=1, keepdims=True)
        cq_sc[...] = jnp.broadcast_to(col, cq_sc.shape)

    def block(diag):
        for qs in range(blk // sq):
            r0 = qs * sq
            rows = pl.ds(r0, sq)
            q = q_ref[rows, :]
            cq = jnp.concatenate([cq_sc[rows, :]] * rep, axis=1)
            m, l, acc = m_sc[rows, :], l_sc[rows, :], acc_sc[rows, :]
            nkc = (r0 + sq - 1) // sk + 1 if diag else blk // sk
            for kc in range(nkc):
                c0 = kc * sk
                cols = pl.ds(c0, sk)
                s = _dot_t(q, k_ref[cols, :]) + cq - ck_ref[0, :, cols]
                if diag and c0 + sk - 1 > r0:
                    row = lax.broadcasted_iota(I32, s.shape, 0) + r0
                    col = lax.broadcasted_iota(I32, s.shape, 1) + c0
                    s = jnp.where(col <= row, s, -jnp.inf)
                m_new = jnp.maximum(m, jnp.max(s, axis=1, keepdims=True))
                alpha = jnp.exp2(m - m_new)
                p = jnp.exp2(s - jnp.concatenate([m_new] * rep, axis=1))
                psum = p[:, :LANES]
                for c in range(1, rep):
                    psum = psum + p[:, c * LANES:(c + 1) * LANES]
                l = alpha * l + psum
                acc = alpha * acc + _dot(p.astype(BF16), v_ref[cols, :])
                m = m_new
            if diag:
                o_ref[rows, :] = (acc / jnp.sum(l, axis=1, keepdims=True)).astype(o_ref.dtype)
            else:
                m_sc[rows, :], l_sc[rows, :], acc_sc[rows, :] = m, l, acc

    @pl.when(ki < qi)
    def _():
        block(False)

    @pl.when(ki == qi)
    def _():
        block(True)


def _fox_attention(proj, cum_col, cum_row, *, batch, length, heads, q_col, k_col, v_col):
    dh = ATTN_HEAD_DIM
    blk = min(ATT_BLOCK, length)
    sq, sk = min(ATT_SQ, blk), min(ATT_SK, blk)
    nq = length // blk
    n = batch * length
    pairs = [(a, b) for a in range(nq) for b in range(a + 1)]
    qi_tab = jnp.asarray([a for a, _ in pairs], I32)
    ki_tab = jnp.asarray([b for _, b in pairs], I32)
    kernel = functools.partial(_fox_kernel, blk=blk, sq=sq, sk=sk)
    grid_spec = pltpu.PrefetchScalarGridSpec(
        num_scalar_prefetch=2,
        grid=(batch, heads, len(pairs)),
        in_specs=[
            pl.BlockSpec((blk, dh), lambda b, h, t, qt, kt: (b * nq + qt[t], q_col + h)),
            pl.BlockSpec((blk, dh), lambda b, h, t, qt, kt: (b * nq + kt[t], k_col + h)),
            pl.BlockSpec((blk, dh), lambda b, h, t, qt, kt: (b * nq + kt[t], v_col + h)),
            pl.BlockSpec((blk, heads), lambda b, h, t, qt, kt: (b * nq + qt[t], 0)),
            pl.BlockSpec((1, 1, blk), lambda b, h, t, qt, kt: (b * heads + h, 0, kt[t])),
        ],
        out_specs=pl.BlockSpec((blk, dh), lambda b, h, t, qt, kt: (b * nq + qt[t], h)),
        scratch_shapes=[pltpu.VMEM((blk, LANES), F32), pltpu.VMEM((blk, LANES), F32),
                        pltpu.VMEM((blk, dh), F32), pltpu.VMEM((blk, LANES), F32)],
    )
    return pl.pallas_call(
        kernel,
        grid_spec=grid_spec,
        out_shape=jax.ShapeDtypeStruct((n, heads * dh), F32),
        compiler_params=_cparams(("parallel", "parallel", "arbitrary"), VMEM_LIMIT),
        name="fox_attention",
    )(qi_tab, ki_tab, proj, proj, proj, cum_col, cum_row)


def _s5_taps_kernel(wk_ref, cc_ref, o_ref):
    o_ref[0] = _dot(wk_ref[0], cc_ref[0], HIGHEST)


def _s5_kernel_taps(wk, cc):
    g, th, p2 = wk.shape
    h = cc.shape[2]
    return pl.pallas_call(
        _s5_taps_kernel,
        grid=(g,),
        in_specs=[pl.BlockSpec((1, th, p2), lambda i: (i, 0, 0)),
                  pl.BlockSpec((1, p2, h), lambda i: (i, 0, 0))],
        out_specs=pl.BlockSpec((1, th, h), lambda i: (i, 0, 0)),
        out_shape=jax.ShapeDtypeStruct((g, th, h), F32),
        compiler_params=_cparams(("parallel",)),
        name="s5_taps",
    )(wk, cc)


def _s5_fused_kernel(u_ref, mi_ref, ms_ref, mo_ref, a1_ref, a2_ref, d_ref, y_ref, tmp_sc, us_sc,
                     *, nlev, chunks, p, t):
    h = SSM_GROUP
    gpb = LANES // h
    n = u_ref.shape[0]
    r = n // t
    nblk = t * h // LANES
    tpb = LANES // h
    rc = tmp_sc.shape[0] // t

    def regroup(c, carry):
        tmp_sc[...] = u_ref[pl.ds(pl.multiple_of(c * rc * t, rc * t), rc * t), :].astype(F32)
        for tau in range(t):
            us_sc[tau, pl.ds(pl.multiple_of(c * rc, rc), rc), :] = tmp_sc[pl.ds(tau, rc, stride=t), :]
        return carry

    lax.fori_loop(0, r // rc, regroup, 0)
    grp = lax.broadcasted_iota(I32, (r, LANES), 1) // h
    row = lax.broadcasted_iota(I32, (r, 2 * p), 0) % chunks

    def group(g, carry):
        blocks = []
        for ob in range(nblk):
            acc = jnp.zeros((r, LANES), F32)
            for q in range(tpb):
                s = us_sc[ob * tpb + q]
                acc = jnp.where(grp == q, pltpu.roll(s, ((q - g) * h) % LANES, axis=1), acc)
            blocks.append(acc)
        uf = jnp.concatenate(blocks, axis=1)
        u = uf.astype(BF16)
        s = _dot(u, ms_ref[g])
        for j in range(nlev):
            d = 1 << j
            sh = jnp.where(row >= d, pltpu.roll(s, d, axis=0), 0.0)
            sw = pltpu.roll(sh, p, axis=1)
            s = s + a1_ref[g, j:j + 1, :] * sh + a2_ref[g, j:j + 1, :] * sw
        sp = jnp.where(row >= 1, pltpu.roll(s, 1, axis=0), 0.0)
        y = _dot(u, mi_ref[g]) + _dot(sp.astype(BF16), mo_ref[g]) + d_ref[g] * u.astype(F32)
        def scatter_lanes(first):
            for ob in range(nblk):
                yb = y[:, ob * LANES:(ob + 1) * LANES]
                for q in range(tpb):
                    tau = ob * tpb + q
                    moved = pltpu.roll(yb, ((g - q) * h) % LANES, axis=1)
                    y_ref[tau] = jnp.where(grp == g, moved, 0.0 if first else y_ref[tau])

        @pl.when(g == 0)
        def _():
            scatter_lanes(True)

        @pl.when(g > 0)
        def _():
            scatter_lanes(False)

        return carry

    lax.fori_loop(0, gpb, group, 0)


def _s5_fused(u, mi, ms, mo, a1, a2, dsk, *, col_blocks, t, chunks):
    n = u.shape[0]
    gpb = LANES // SSM_GROUP
    th = t * SSM_GROUP
    p2 = ms.shape[2]
    nlev = a1.shape[1]
    r = n // t
    kernel = functools.partial(_s5_fused_kernel, nlev=nlev, chunks=chunks, p=p2 // 2, t=t)
    blk3 = lambda i: (i, 0, 0)
    return pl.pallas_call(
        kernel,
        grid=(col_blocks,),
        in_specs=[pl.BlockSpec((n, LANES), lambda i: (0, i)),
                  pl.BlockSpec((gpb, th, th), blk3),
                  pl.BlockSpec((gpb, th, p2), blk3),
                  pl.BlockSpec((gpb, p2, th), blk3),
                  pl.BlockSpec((gpb, nlev, p2), blk3),
                  pl.BlockSpec((gpb, nlev, p2), blk3),
                  pl.BlockSpec((gpb, 1, th), blk3)],
        out_specs=pl.BlockSpec((t, r, LANES), lambda i: (0, 0, i)),
        out_shape=jax.ShapeDtypeStruct((t, r, col_blocks * LANES), F32),
        scratch_shapes=[pltpu.VMEM((min(r, S5_REGROUP_ROWS) * t, LANES), F32), pltpu.VMEM((t, r, LANES), F32)],
        compiler_params=_cparams(("parallel",), VMEM_LIMIT),
        name="s5_mixer",
    )(u, mi, ms, mo, a1, a2, dsk)


def _s5_params(lam_re, lam_im, log_step, b_re, b_im, c_re, c_im, d_skip, t, chunks):
    g, p = lam_re.shape
    h = SSM_GROUP
    lr, li = lam_re.astype(F32), lam_im.astype(F32)
    dt = jnp.exp(log_step.astype(F32))[:, None]
    mag = jnp.exp(lr * dt)
    lb_re, lb_im = mag * jnp.cos(li * dt), mag * jnp.sin(li * dt)
    den = lr * lr + li * li
    nr, ni = lb_re - 1.0, lb_im
    coef_re = (nr * lr + ni * li) / den
    coef_im = (ni * lr - nr * li) / den
    br, bi = b_re.astype(F32), b_im.astype(F32)
    bb_re = coef_re[..., None] * br - coef_im[..., None] * bi
    bb_im = coef_re[..., None] * bi + coef_im[..., None] * br

    def power(k):
        return jnp.exp(lr * dt * k) * jnp.cos(li * dt * k), jnp.exp(lr * dt * k) * jnp.sin(li * dt * k)

    kk = jnp.arange(t + 1, dtype=F32)[:, None, None]
    pw_re, pw_im = power(kk)
    wb_re = pw_re[:t, ..., None] * bb_re - pw_im[:t, ..., None] * bb_im
    wb_im = pw_re[:t, ..., None] * bb_im + pw_im[:t, ..., None] * bb_re
    wk = jnp.concatenate([wb_re, wb_im], axis=2).transpose(1, 0, 3, 2)
    ms = wk[:, ::-1].reshape(g, t * h, 2 * p)
    cre, cim = c_re.astype(F32), c_im.astype(F32)
    cc = jnp.concatenate([cre.transpose(0, 2, 1), -cim.transpose(0, 2, 1)], axis=1)
    taps = _s5_kernel_taps(wk.reshape(g, t * h, 2 * p), cc)
    kt = taps.reshape(g, t, h, h).transpose(0, 2, 1, 3).reshape(g, h, t * h)
    kpad = jnp.concatenate([jnp.zeros_like(kt), kt], axis=2)
    mi = jnp.stack([kpad[:, :, (t - s) * h:(2 * t - s) * h] for s in range(t)], axis=1)
    mi = mi.reshape(g, t * h, t * h)
    cre_t, cim_t = cre.transpose(0, 2, 1)[:, :, None, :], cim.transpose(0, 2, 1)[:, :, None, :]
    are = pw_re[1:].transpose(1, 2, 0)[..., None]
    aim = pw_im[1:].transpose(1, 2, 0)[..., None]
    mo_re = cre_t * are - cim_t * aim
    mo_im = -(cre_t * aim + cim_t * are)
    mo = jnp.concatenate([mo_re, mo_im], axis=1).reshape(g, 2 * p, t * h)
    nlev = max(1, (chunks - 1).bit_length())
    lv = (float(t) * (2.0 ** jnp.arange(nlev, dtype=F32)))[None, :, None]
    ar, ai = _power_levels(lr, li, dt, lv)
    a1 = jnp.concatenate([ar, ar], axis=2)
    a2 = jnp.concatenate([-ai, ai], axis=2)
    dsk = jnp.tile(d_skip.astype(F32).reshape(g, 1, h), (1, 1, t))
    return mi.astype(BF16), ms.astype(BF16), mo.astype(BF16), a1, a2, dsk


def _power_levels(lr, li, dt, lv):
    e = (lr * dt)[:, None, :] * lv
    w = (li * dt)[:, None, :] * lv
    return jnp.exp(e) * jnp.cos(w), jnp.exp(e) * jnp.sin(w)


def _mix_out_kernel(y_ref, att_ref, x_ref, wglu_ref, bglu_ref, gs_ref, ga_ref, woa_ref, wos_ref, o_ref, y_sc):
    t = y_ref.shape[0]
    for rl in range(y_ref.shape[1]):
        y_sc[rl * t:(rl + 1) * t, :] = y_ref[:, rl, :]
    y = jax.nn.gelu(y_sc[...])
    z = _dot(y.astype(BF16), wglu_ref[...]) + bglu_ref[...]
    ssm = y * jax.nn.sigmoid(z)
    na = _rms(att_ref[...], ga_ref[...]).astype(BF16)
    ns = _rms(ssm, gs_ref[...]).astype(BF16)
    o_ref[...] = x_ref[...] + _dot(na, woa_ref[...]) + _dot(ns, wos_ref[...])


def _mix_out(ypre, att, x, w_glu, b_glu, g_ssm, g_att, wo_a, wo_s, tm):
    n, d = x.shape
    t, _, sw = ypre.shape
    aw = att.shape[1]
    tm = min(tm, n)
    row = lambda i: (i, 0)
    fix = lambda i: (0, 0)
    once = dict(pipeline_mode=pl.Buffered(1))
    return pl.pallas_call(
        _mix_out_kernel,
        grid=(n // tm,),
        in_specs=[pl.BlockSpec((t, tm // t, sw), lambda i: (0, i, 0)),
                  pl.BlockSpec((tm, aw), row), pl.BlockSpec((tm, d), row),
                  pl.BlockSpec((sw, sw), fix, **once), pl.BlockSpec((1, sw), fix),
                  pl.BlockSpec((1, sw), fix), pl.BlockSpec((1, aw), fix),
                  pl.BlockSpec((aw, d), fix, **once), pl.BlockSpec((sw, d), fix, **once)],
        out_specs=pl.BlockSpec((tm, d), row),
        out_shape=jax.ShapeDtypeStruct((n, d), F32),
        scratch_shapes=[pltpu.VMEM((tm, sw), F32)],
        compiler_params=_cparams(("parallel",), VMEM_LIMIT),
        name="mix_out",
    )(ypre, att, x, w_glu, b_glu.reshape(1, sw), g_ssm.reshape(1, sw), g_att.reshape(1, aw), wo_a, wo_s)


def _cross_kernel(h_ref, g_ref, wq_ref, kv_ref, wo_ref, o_ref, *, heads):
    h = h_ref[...]
    d = h.shape[1]
    dh = d // heads
    q = _dot(_rms(h, g_ref[...]).astype(BF16), wq_ref[...]).astype(BF16)
    out = h
    for hd in range(heads):
        lo = hd * dh
        s = _dot_t(q[:, lo:lo + dh], kv_ref[:, lo:lo + dh]) * dh ** -0.5
        e = jnp.exp(s - jnp.max(s, axis=1, keepdims=True))
        p = e / jnp.sum(e, axis=1, keepdims=True)
        oh = _dot(p.astype(BF16), kv_ref[:, d + lo:d + lo + dh])
        out = out + _dot(oh.astype(BF16), wo_ref[lo:lo + dh, :])
    o_ref[...] = out


def _cross_attention(h, g, wq, kv, wo, *, batch, mem_len, tm):
    n, d = h.shape
    tm = min(tm, n // batch)
    tiles = n // batch // tm
    fix = lambda i: (0, 0)
    once = dict(pipeline_mode=pl.Buffered(1))
    return pl.pallas_call(
        functools.partial(_cross_kernel, heads=CROSS_HEADS),
        grid=(n // tm,),
        in_specs=[pl.BlockSpec((tm, d), lambda i: (i, 0)),
                  pl.BlockSpec((1, d), fix),
                  pl.BlockSpec((d, d), fix, **once),
                  pl.BlockSpec((mem_len, 2 * d), lambda i: (i // tiles, 0)),
                  pl.BlockSpec((d, d), fix, **once)],
        out_specs=pl.BlockSpec((tm, d), lambda i: (i, 0)),
        out_shape=jax.ShapeDtypeStruct((n, d), F32),
        compiler_params=_cparams(("parallel",), VMEM_LIMIT),
        name="cross_attention",
    )(h, g.reshape(1, d), wq, kv, wo)


def _router_kernel(h_ref, g_ref, wr_ref, br_ref, xm_ref, idx_ref, gate_ref, rank_ref, cnt_ref, carry_sc):
    @pl.when(pl.program_id(0) == 0)
    def _():
        carry_sc[...] = jnp.zeros(carry_sc.shape, F32)

    hm = _rms(h_ref[...], g_ref[...])
    xm_ref[...] = hm
    logits = _dot(hm, wr_ref[...], HIGHEST) + br_ref[...]
    tm = logits.shape[0]
    lane = lax.broadcasted_iota(I32, logits.shape, 1).astype(F32)
    work = logits
    vals, idxs = [], []
    for _ in range(TOP_K):
        m = jnp.max(work, axis=1, keepdims=True)
        am = jnp.min(jnp.where(work == m, lane, float(LANES)), axis=1, keepdims=True)
        vals.append(m)
        idxs.append(am)
        work = jnp.where(lane == am, NEG_BIG, work)
    es = [jnp.exp(v - vals[0]) for v in vals]
    den = es[0] + es[1] + es[2] + es[3]
    sel = jnp.zeros(logits.shape, F32)
    for am in idxs:
        sel = sel + (lane == am).astype(F32)
    r = lax.broadcasted_iota(I32, (tm, tm), 0)
    c = lax.broadcasted_iota(I32, (tm, tm), 1)
    before = (c < r).astype(BF16)
    rank_all = carry_sc[0:1, :] + _dot(before, sel.astype(BF16))
    idx_o = jnp.zeros(logits.shape, F32)
    gate_o = jnp.zeros(logits.shape, F32)
    rank_o = jnp.zeros(logits.shape, F32)
    for j in range(TOP_K):
        slot = lane == float(j)
        rk = jnp.sum(jnp.where(lane == idxs[j], rank_all, 0.0), axis=1, keepdims=True)
        idx_o = jnp.where(slot, idxs[j], idx_o)
        gate_o = jnp.where(slot, es[j] / den, gate_o)
        rank_o = jnp.where(slot, rk, rank_o)
    idx_ref[...] = idx_o.astype(I32)
    gate_ref[...] = gate_o
    rank_ref[...] = rank_o.astype(I32)
    carry_sc[...] = carry_sc[...] + jnp.sum(sel, axis=0, keepdims=True)
    cnt_ref[...] = carry_sc[...].astype(I32)


def _router(h, g, wr, br, tm):
    n, d = h.shape
    tm = min(tm, n)
    row = lambda i: (i, 0)
    fix = lambda i: (0, 0)
    return pl.pallas_call(
        _router_kernel,
        grid=(n // tm,),
        in_specs=[pl.BlockSpec((tm, d), row), pl.BlockSpec((1, d), fix),
                  pl.BlockSpec((d, LANES), fix), pl.BlockSpec((1, LANES), fix)],
        out_specs=[pl.BlockSpec((tm, d), row), pl.BlockSpec((tm, LANES), row),
                   pl.BlockSpec((tm, LANES), row), pl.BlockSpec((tm, LANES), row),
                   pl.BlockSpec((8, LANES), fix)],
        out_shape=[jax.ShapeDtypeStruct((n, d), F32), jax.ShapeDtypeStruct((n, LANES), I32),
                   jax.ShapeDtypeStruct((n, LANES), F32), jax.ShapeDtypeStruct((n, LANES), I32),
                   jax.ShapeDtypeStruct((8, LANES), I32)],
        scratch_shapes=[pltpu.VMEM((8, LANES), F32)],
        compiler_params=_cparams(("arbitrary",), VMEM_LIMIT),
        name="router",
    )(h, g.reshape(1, d), wr, br)


def _row_copy(src_hbm, row, dst, slot, sem):
    return pltpu.make_async_copy(src_hbm.at[pl.ds(row, 1), :], dst.at[pl.ds(slot, 1), :], sem)


def _slab_copy(src_hbm, dst, slot, rows, sem):
    return pltpu.make_async_copy(src_hbm.at[pl.ds(0, rows), :], dst.at[pl.ds(slot, rows), :], sem)


def _expert_kernel(sbe_ref, sbn_ref, sbf_ref, nused_ref, tok_ref, tokn_ref, xm_hbm, wgu_hbm, wdn_hbm, bgu_ref, bdn_ref,
                    o_ref, xg_ref, xb_ref, wg_res, wl_res, wd_res, stg_g, stg_l, stg_d, gsem, wsem,
                    *, nf, tf, n_sb):
    i = pl.program_id(0)
    slot = i % 2
    e = sbe_ref[i]
    nsub = (sbn_ref[i] + MOE_SUB - 1) // MOE_SUB
    first = sbf_ref[i]
    sb_rows, d = xb_ref.shape
    subs = sb_rows // MOE_SUB
    step_rows = sb_rows // nf

    def weight_copies(expert, j, s):
        lo = pl.multiple_of(j * tf, tf)
        hi = pl.multiple_of(nf * tf + j * tf, tf)
        return (pltpu.make_async_copy(wgu_hbm.at[expert, :, pl.ds(lo, tf)], stg_g.at[s], wsem.at[0, s]),
                pltpu.make_async_copy(wgu_hbm.at[expert, :, pl.ds(hi, tf)], stg_l.at[s], wsem.at[1, s]),
                pltpu.make_async_copy(wdn_hbm.at[expert, pl.ds(lo, tf), :], stg_d.at[s], wsem.at[2, s]))

    def start_first_tiles(expert):
        for s in range(2):
            for cp in weight_copies(expert, s, s):
                cp.start()

    nxt = jnp.minimum(i + 1, n_sb - 1)
    nsub_next = jnp.where(i + 1 < n_sb, (sbn_ref[nxt] + MOE_SUB - 1) // MOE_SUB, 0)
    next_is_first = jnp.logical_and(i + 1 < n_sb, sbf_ref[nxt] == 1)

    @pl.when(i == 0)
    def _():
        start_first_tiles(e)

        def issue(r8, c):
            for u in range(ISSUE_UNROLL):
                r = r8 * ISSUE_UNROLL + u
                _row_copy(xm_hbm, tok_ref[0, 0, r], xg_ref.at[0], r, gsem).start()
            return c

        lax.fori_loop(0, nsub * (MOE_SUB // ISSUE_UNROLL), issue, 0)

    for s in range(subs):
        @pl.when(s < nsub)
        def _():
            _slab_copy(xm_hbm, xg_ref.at[slot], s * MOE_SUB, MOE_SUB, gsem).wait()

    for s in range(subs):
        @pl.when(s < nsub)
        def _():
            rows = pl.ds(s * MOE_SUB, MOE_SUB)
            xb_ref[rows, :] = xg_ref[slot, rows, :].astype(BF16)

    o_ref[...] = jnp.broadcast_to(bdn_ref[...], o_ref.shape)

    @pl.when(jnp.logical_and(next_is_first, first == 0))
    def _():
        start_first_tiles(sbe_ref[nxt])

    def gather_next(j):
        @pl.when((j * step_rows) // MOE_SUB < nsub_next)
        def _():
            for u in range(step_rows):
                r = j * step_rows + u
                _row_copy(xm_hbm, tokn_ref[0, 0, r], xg_ref.at[1 - slot], r, gsem).start()

    def ffn(j, rows):
        x = xb_ref[rows, :]
        g = jnp.minimum(_dot(x, wg_res[j]) + bgu_ref[j], SWIGLU_LIMIT)
        lin = jnp.clip(_dot(x, wl_res[j]) + bgu_ref[nf + j], -SWIGLU_LIMIT, SWIGLU_LIMIT)
        act = (g * jax.nn.sigmoid(SWIGLU_ALPHA * g) * (lin + 1.0)).astype(BF16)
        for c in range(0, d, MOE_NCHUNK):
            cols = pl.ds(c, min(MOE_NCHUNK, d - c))
            o_ref[rows, cols] = o_ref[rows, cols] + _dot(act, wd_res[j, :, cols])

    def tile(j, carry):
        s = j % 2

        @pl.when(first == 1)
        def _():
            for cp in weight_copies(e, j, s):
                cp.wait()
            wg_res[j] = stg_g[s].astype(BF16)
            wl_res[j] = stg_l[s].astype(BF16)
            wd_res[j] = stg_d[s].astype(BF16)

            @pl.when(j + 2 < nf)
            def _():
                for cp in weight_copies(e, j + 2, s):
                    cp.start()

        gather_next(j)
        for k in range(1, subs + 1):
            @pl.when(nsub == k)
            def _():
                ffn(j, pl.ds(0, k * MOE_SUB))

        return carry

    lax.fori_loop(0, nf, tile, 0)

    @pl.when(jnp.logical_and(next_is_first, first == 1))
    def _():
        start_first_tiles(sbe_ref[nxt])


def _experts(xm, row_tok, sb_e, sb_n, sb_f, n_used, w_gu, b_gu, w_dn, b_dn):
    n_sb = sb_e.shape[0]
    sb_rows = MOE_SB
    e, d, f2 = w_gu.shape
    f = f2 // 2
    tf = min(MOE_FTILE, f)
    nf = f // tf
    assert nf >= 2 and sb_rows % nf == 0 and MOE_SUB % (sb_rows // nf) == 0
    tok3 = row_tok.reshape(n_sb, 1, sb_rows)
    grid_spec = pltpu.PrefetchScalarGridSpec(
        num_scalar_prefetch=4,
        grid=(n_sb,),
        in_specs=[
            pl.BlockSpec((1, 1, sb_rows), lambda i, sbe, sbn, sbf, nu: (i, 0, 0), memory_space=pltpu.SMEM),
            pl.BlockSpec((1, 1, sb_rows), lambda i, sbe, sbn, sbf, nu: (jnp.minimum(i + 1, n_sb - 1), 0, 0),
                         memory_space=pltpu.SMEM),
            pl.BlockSpec(memory_space=pl.ANY),
            pl.BlockSpec(memory_space=pl.ANY),
            pl.BlockSpec(memory_space=pl.ANY),
            pl.BlockSpec((None, 2 * nf, 1, tf), lambda i, sbe, sbn, sbf, nu: (sbe[i], 0, 0, 0)),
            pl.BlockSpec((None, 1, d), lambda i, sbe, sbn, sbf, nu: (sbe[i], 0, 0)),
        ],
        out_specs=pl.BlockSpec((sb_rows, d), lambda i, sbe, sbn, sbf, nu: (jnp.where(i < nu[0], i, n_sb), 0)),
        scratch_shapes=[pltpu.VMEM((2, sb_rows, d), F32), pltpu.VMEM((sb_rows, d), BF16),
                        pltpu.VMEM((nf, d, tf), BF16), pltpu.VMEM((nf, d, tf), BF16), pltpu.VMEM((nf, tf, d), BF16),
                        pltpu.VMEM((2, d, tf), F32), pltpu.VMEM((2, d, tf), F32), pltpu.VMEM((2, tf, d), F32),
                        pltpu.SemaphoreType.DMA, pltpu.SemaphoreType.DMA((3, 2))],
    )
    return pl.pallas_call(
        functools.partial(_expert_kernel, nf=nf, tf=tf, n_sb=n_sb),
        grid_spec=grid_spec,
        out_shape=jax.ShapeDtypeStruct(((n_sb + 1) * sb_rows, d), F32),
        compiler_params=_cparams(("arbitrary",), MOE_VMEM_LIMIT),
        name="experts",
    )(sb_e, sb_n, sb_f, n_used, tok3, tok3, xm, w_gu, w_dn, b_gu.reshape(e, 2 * nf, 1, tf), b_dn.reshape(e, 1, d))


def _combine_kernel(dest_ref, gate_ref, h_ref, g_ref, y_hbm, o_ref, buf_ref, sem):
    tm = h_ref.shape[0]

    tok_unroll = ISSUE_UNROLL // TOP_K

    def issue(t2, c):
        for u in range(tok_unroll):
            t = t2 * tok_unroll + u
            for k in range(TOP_K):
                _row_copy(y_hbm, dest_ref[0, 0, t * TOP_K + k], buf_ref.at[k], t, sem).start()
        return c

    lax.fori_loop(0, tm // tok_unroll, issue, 0)
    acc = h_ref[...]
    gates = gate_ref[...]
    for k in range(TOP_K):
        _slab_copy(y_hbm, buf_ref.at[k], 0, tm, sem).wait()
    for k in range(TOP_K):
        acc = acc + gates[:, k:k + 1] * buf_ref[k]
    o_ref[...] = _rms(acc, g_ref[...])


def _combine(dest, gates, h, g, yrows, tm):
    n, d = h.shape
    tm = min(tm, n)
    return pl.pallas_call(
        _combine_kernel,
        grid=(n // tm,),
        in_specs=[pl.BlockSpec((1, 1, tm * TOP_K), lambda i: (i, 0, 0), memory_space=pltpu.SMEM),
                  pl.BlockSpec((tm, LANES), lambda i: (i, 0)),
                  pl.BlockSpec((tm, d), lambda i: (i, 0)),
                  pl.BlockSpec((1, d), lambda i: (0, 0)),
                  pl.BlockSpec(memory_space=pl.ANY)],
        out_specs=pl.BlockSpec((tm, d), lambda i: (i, 0)),
        out_shape=jax.ShapeDtypeStruct((n, d), F32),
        scratch_shapes=[pltpu.VMEM((TOP_K, tm, d), F32), pltpu.SemaphoreType.DMA],
        compiler_params=_cparams(("arbitrary",), VMEM_LIMIT),
        name="combine",
    )(dest.reshape(n // tm, 1, tm * TOP_K), gates, h, g.reshape(1, d), yrows)


def _layer(h, mem2d, p, *, batch, length):
    n, d = h.shape
    g_cnt, p_state = p["lam_re"].shape
    sw = g_cnt * SSM_GROUP
    aw = d - sw
    heads = aw // ATTN_HEAD_DIM
    assert p["w_in"].shape[1] == sw + 3 * aw + heads and sw % LANES == 0 and 2 * p_state == LANES

    w_in = p["w_in"]
    cols = sw + 3 * aw
    q_scale = ATTN_HEAD_DIM ** -0.5 * LOG2E
    cs = jnp.concatenate([jnp.ones((sw,), F32), jnp.full((aw,), q_scale, F32), jnp.ones((2 * aw,), F32)])
    w_f = jnp.pad(w_in[:, cols:], ((0, 0), (0, LANES - heads)))
    proj, flog = _in_proj(h, p["g_mix"], w_in[:, :cols].astype(BF16), cs, w_f, 1024, 1024)

    f_rows = flog[:, :heads].reshape(batch, length, heads).transpose(0, 2, 1).reshape(batch * heads, length)
    bias_rows = jnp.tile(p["b_f"].astype(F32), batch).reshape(batch * heads, 1)
    cum_rows = _fgate_cumsum(f_rows, bias_rows)
    cum_col = cum_rows.reshape(batch, heads, length).transpose(0, 2, 1).reshape(n, heads)

    blk = lambda c: c // ATTN_HEAD_DIM
    att = _fox_attention(proj, cum_col, cum_rows.reshape(batch * heads, 1, length), batch=batch, length=length,
                         heads=heads, q_col=blk(sw), k_col=blk(sw + aw), v_col=blk(sw + 2 * aw))

    t = min(S5_CHUNK, length)
    chunks = length // t
    mi, ms, mo, a1, a2, dsk = _s5_params(p["lam_re"], p["lam_im"], p["log_step"], p["b_re"], p["b_im"],
                                         p["c_re"], p["c_im"], p["d_skip"], t, chunks)
    ypre = _s5_fused(proj, mi, ms, mo, a1, a2, dsk, col_blocks=sw // LANES, t=t, chunks=chunks)

    w_out = p["w_out"].astype(BF16)
    h = _mix_out(ypre, att, h, p["w_glu"].astype(BF16), p["b_glu"], p["g_ssm_out"], p["g_attn_out"],
                 w_out[:aw], w_out[aw:], 256)

    mem_len = mem2d.shape[0] // batch
    kv = _norm_matmul(mem2d, p["g_mem"], p["w_ckv"].astype(BF16), BF16, 512, 512)
    h = _cross_attention(h, p["g_cross"], p["w_cq"].astype(BF16), kv, p["w_co"].astype(BF16),
                         batch=batch, mem_len=mem_len, tm=512)

    dest, gates, yrows = _moe(h, p)
    return dest, gates, h, yrows


def _moe(h, p):
    n = h.shape[0]
    n_exp = p["w_router"].shape[1]
    wr = jnp.pad(p["w_router"].astype(F32), ((0, 0), (0, LANES - n_exp)))
    br = jnp.pad(p["b_router"].astype(F32), (0, LANES - n_exp), constant_values=NEG_BIG).reshape(1, LANES)
    xm, idx, gates, rank, cnt = _router(h, p["g_moe"], wr, br, 256)

    sb_rows = MOE_SB
    n_sb = (n * TOP_K) // sb_rows + n_exp
    counts = cnt[0, :n_exp]
    padded = (counts + sb_rows - 1) // sb_rows * sb_rows
    pend = jnp.cumsum(padded)
    pstart = pend - padded
    idx_k, rank_k = idx[:, :TOP_K], rank[:, :TOP_K]
    onehot = idx_k[..., None] == jnp.arange(n_exp, dtype=I32)
    dest = rank_k + jnp.sum(jnp.where(onehot, pstart, 0), axis=-1)
    tok = jnp.broadcast_to(jnp.arange(n, dtype=I32)[:, None], (n, TOP_K))
    row_tok = jnp.zeros((n_sb * sb_rows,), I32).at[dest.reshape(-1)].set(tok.reshape(-1), unique_indices=True)
    n_used = (pend[-1] // sb_rows).astype(I32)
    sb_start = jnp.arange(n_sb, dtype=I32) * sb_rows
    sb_e = jnp.clip(jnp.sum(sb_start[:, None] >= pend[None, :], axis=1), 0, n_exp - 1).astype(I32)
    used = jnp.arange(n_sb, dtype=I32) < n_used
    sb_n = jnp.where(used, jnp.clip(counts[sb_e] - (sb_start - pstart[sb_e]), 0, sb_rows), 0).astype(I32)
    sb_f = jnp.logical_and(used, sb_start == pstart[sb_e]).astype(I32)
    sb_e = jnp.where(used, sb_e, sb_e[jnp.maximum(n_used - 1, 0)])

    yrows = _experts(xm, row_tok, sb_e, sb_n, sb_f, n_used.reshape(1), p["w_gu"], p["b_gu"], p["w_dn"], p["b_dn"])
    return dest, gates, yrows


def kernel(x, mem, g_mix, w_in, b_f, lam_re, lam_im, log_step, b_re, b_im, c_re, c_im, d_skip, w_glu, b_glu, g_attn_out, g_ssm_out, w_out, g_cross, g_mem, w_cq, w_ckv, w_co, g_moe, w_router, b_router, w_gu, b_gu, w_dn, b_dn, g_final):
    batch, length, d = x.shape
    depth = g_mix.shape[0]
    assert depth == 1, "the combine kernel applies the final norm, so exactly one layer is supported"
    names = ("g_mix", "w_in", "b_f", "lam_re", "lam_im", "log_step", "b_re", "b_im", "c_re", "c_im", "d_skip",
             "w_glu", "b_glu", "g_attn_out", "g_ssm_out", "w_out", "g_cross", "g_mem", "w_cq", "w_ckv", "w_co",
             "g_moe", "w_router", "b_router", "w_gu", "b_gu", "w_dn", "b_dn")
    vals = (g_mix, w_in, b_f, lam_re, lam_im, log_step, b_re, b_im, c_re, c_im, d_skip, w_glu, b_glu,
            g_attn_out, g_ssm_out, w_out, g_cross, g_mem, w_cq, w_ckv, w_co, g_moe, w_router, b_router,
            w_gu, b_gu, w_dn, b_dn)
    params = {k: v[0] for k, v in zip(names, vals)}
    h = x.reshape(batch * length, d)
    mem2d = mem.reshape(-1, d)
    dest, gates, h, yrows = _layer(h, mem2d, params, batch=batch, length=length)
    out = _combine(dest, gates, h, g_final, yrows, 256)
    return out.reshape(batch, length, d)
```

```python
import functools

import jax
import jax.numpy as jnp
from jax import lax
from jax.experimental import pallas as pl
from jax.experimental.pallas import tpu as pltpu

F32 = jnp.float32
BF16 = jnp.bfloat16
I32 = jnp.int32

RMS_EPS = 1e-5
SSM_GROUP = 16
ATTN_HEAD_DIM = 128
CROSS_HEADS = 4
TOP_K = 4
SWIGLU_ALPHA = 1.702
SWIGLU_LIMIT = 7.0

LANES = 128
LOG2E = 1.4426950408889634
ATT_BLOCK = 2048
ATT_SQ = 128
ATT_SK = 256
S5_CHUNK = 32
S5_REGROUP_ROWS = 64
MOE_SB = 512
MOE_SUB = 256
MOE_FTILE = 256
MOE_NCHUNK = 1024
ISSUE_UNROLL = 8
NEG_BIG = -1e30
VMEM_LIMIT = 56 * 1024 * 1024
MOE_VMEM_LIMIT = 60 * 1024 * 1024

HIGHEST = lax.Precision.HIGHEST


def _cparams(sem, vmem=None):
    return pltpu.CompilerParams(dimension_semantics=sem, vmem_limit_bytes=vmem)


def _rms(x, g):
    return x * lax.rsqrt(jnp.mean(x * x, axis=-1, keepdims=True) + RMS_EPS) * g


def _dot(a, b, precision=None):
    return jnp.dot(a, b, preferred_element_type=F32, precision=precision)


def _dot_t(a, b):
    return lax.dot_general(a, b, (((1,), (1,)), ((), ())), preferred_element_type=F32)


def _norm_matmul_kernel(x_ref, g_ref, w_ref, o_ref, xn_ref):
    @pl.when(pl.program_id(1) == 0)
    def _():
        xn_ref[...] = _rms(x_ref[...], g_ref[...]).astype(xn_ref.dtype)

    o_ref[...] = _dot(xn_ref[...], w_ref[...]).astype(o_ref.dtype)


def _norm_matmul(x, g, w, out_dtype, tm, tn):
    n, k = x.shape
    m = w.shape[1]
    tm, tn = min(tm, n), min(tn, m)
    return pl.pallas_call(
        _norm_matmul_kernel,
        grid=(n // tm, m // tn),
        in_specs=[pl.BlockSpec((tm, k), lambda i, j: (i, 0)),
                  pl.BlockSpec((1, k), lambda i, j: (0, 0)),
                  pl.BlockSpec((k, tn), lambda i, j: (0, j))],
        out_specs=pl.BlockSpec((tm, tn), lambda i, j: (i, j)),
        out_shape=jax.ShapeDtypeStruct((n, m), out_dtype),
        scratch_shapes=[pltpu.VMEM((tm, k), w.dtype)],
        compiler_params=_cparams(("parallel", "arbitrary"), VMEM_LIMIT),
        name="norm_matmul",
    )(x, g.reshape(1, k), w)


def _in_proj_kernel(x_ref, g_ref, w_ref, cs_ref, wf_ref, o_ref, f_ref, xn_ref):
    @pl.when(pl.program_id(1) == 0)
    def _():
        xn = _rms(x_ref[...], g_ref[...])
        xn_ref[...] = xn.astype(xn_ref.dtype)
        f_ref[...] = _dot(xn, wf_ref[...], HIGHEST)

    o_ref[...] = (_dot(xn_ref[...], w_ref[...]) * cs_ref[...]).astype(o_ref.dtype)


def _in_proj(x, g, w, col_scale, w_f, tm, tn):
    n, k = x.shape
    m = w.shape[1]
    tm, tn = min(tm, n), min(tn, m)
    return pl.pallas_call(
        _in_proj_kernel,
        grid=(n // tm, m // tn),
        in_specs=[pl.BlockSpec((tm, k), lambda i, j: (i, 0)),
                  pl.BlockSpec((1, k), lambda i, j: (0, 0)),
                  pl.BlockSpec((k, tn), lambda i, j: (0, j)),
                  pl.BlockSpec((1, tn), lambda i, j: (0, j)),
                  pl.BlockSpec((k, LANES), lambda i, j: (0, 0))],
        out_specs=[pl.BlockSpec((tm, tn), lambda i, j: (i, j)),
                   pl.BlockSpec((tm, LANES), lambda i, j: (i, 0))],
        out_shape=[jax.ShapeDtypeStruct((n, m), BF16), jax.ShapeDtypeStruct((n, LANES), F32)],
        scratch_shapes=[pltpu.VMEM((tm, k), BF16)],
        compiler_params=_cparams(("parallel", "arbitrary"), VMEM_LIMIT),
        name="in_proj",
    )(x, g.reshape(1, k), w, col_scale.reshape(1, m), w_f)


def _fgate_cumsum_kernel(f_ref, b_ref, o_ref):
    z = f_ref[...] + b_ref[...]
    x = jnp.minimum(z, 0.0) - jnp.log(1.0 + jnp.exp(-jnp.abs(z)))
    lane = lax.broadcasted_iota(I32, x.shape, 1)
    d = 1
    while d < x.shape[1]:
        x = x + jnp.where(lane >= d, pltpu.roll(x, d, axis=1), 0.0)
        d *= 2
    o_ref[...] = x * LOG2E


def _fgate_cumsum(f_rows, bias_rows):
    return pl.pallas_call(
        _fgate_cumsum_kernel,
        out_shape=jax.ShapeDtypeStruct(f_rows.shape, F32),
        name="fgate_cumsum",
    )(f_rows, bias_rows)


def _fox_kernel(qi_ref, ki_ref, q_ref, k_ref, v_ref, cq_ref, ck_ref, o_ref, m_sc, l_sc, acc_sc, cq_sc,
                *, blk, sq, sk):
    h = pl.program_id(1)
    t = pl.program_id(2)
    qi = qi_ref[t]
    ki = ki_ref[t]
    rep = sk // LANES

    @pl.when(ki == 0)
    def _():
        m_sc[...] = jnp.full(m_sc.shape, -jnp.inf, F32)
        l_sc[...] = jnp.zeros(l_sc.shape, F32)
        acc_sc[...] = jnp.zeros(acc_sc.shape, F32)
        cq = cq_ref[...]
        lane = lax.broadcasted_iota(I32, cq.shape, 1)
        col = jnp.sum(jnp.where(lane == h, cq, 0.0), axis=1, keepdims=True)
        cq_sc[...] = jnp.broadcast_to(col, cq_sc.shape)

    def block(diag):
        for qs in range(blk // sq):
            r0 = qs * sq
            rows = pl.ds(r0, sq)
            q = q_ref[rows, :]
            cq = jnp.concatenate([cq_sc[rows, :]] * rep, axis=1)
            m, l, acc = m_sc[rows, :], l_sc[rows, :], acc_sc[rows, :]
            nkc = (r0 + sq - 1) // sk + 1 if diag else blk // sk
            for kc in range(nkc):
                c0 = kc * sk
                cols = pl.ds(c0, sk)
                s = _dot_t(q, k_ref[cols, :]) + cq - ck_ref[0, :, cols]
                if diag and c0 + sk - 1 > r0:
                    row = lax.broadcasted_iota(I32, s.shape, 0) + r0
                    col = lax.broadcasted_iota(I32, s.shape, 1) + c0
                    s = jnp.where(col <= row, s, -jnp.inf)
                m_new = jnp.maximum(m, jnp.max(s, axis=1, keepdims=True))
                alpha = jnp.exp2(m - m_new)
                p = jnp.exp2(s - jnp.concatenate([m_new] * rep, axis=1))
                psum = p[:, :LANES]
                for c in range(1, rep):
                    psum = psum + p[:, c * LANES:(c + 1) * LANES]
                l = alpha * l + psum
                acc = alpha * acc + _dot(p.astype(BF16), v_ref[cols, :])
                m = m_new
            if diag:
                o_ref[rows, :] = (acc / jnp.sum(l, axis=1, keepdims=True)).astype(o_ref.dtype)
            else:
                m_sc[rows, :], l_sc[rows, :], acc_sc[rows, :] = m, l, acc

    @pl.when(ki < qi)
    def _():
        block(False)

    @pl.when(ki == qi)
    def _():
        block(True)


def _fox_attention(proj, cum_col, cum_row, *, batch, length, heads, q_col, k_col, v_col):
    dh = ATTN_HEAD_DIM
    blk = min(ATT_BLOCK, length)
    sq, sk = min(ATT_SQ, blk), min(ATT_SK, blk)
    nq = length // blk
    n = batch * length
    pairs = [(a, b) for a in range(nq) for b in range(a + 1)]
    qi_tab = jnp.asarray([a for a, _ in pairs], I32)
    ki_tab = jnp.asarray([b for _, b in pairs], I32)
    kernel = functools.partial(_fox_kernel, blk=blk, sq=sq, sk=sk)
    grid_spec = pltpu.PrefetchScalarGridSpec(
        num_scalar_prefetch=2,
        grid=(batch, heads, len(pairs)),
        in_specs=[
            pl.BlockSpec((blk, dh), lambda b, h, t, qt, kt: (b * nq + qt[t], q_col + h)),
            pl.BlockSpec((blk, dh), lambda b, h, t, qt, kt: (b * nq + kt[t], k_col + h)),
            pl.BlockSpec((blk, dh), lambda b, h, t, qt, kt: (b * nq + kt[t], v_col + h)),
            pl.BlockSpec((blk, heads), lambda b, h, t, qt, kt: (b * nq + qt[t], 0)),
            pl.BlockSpec((1, 1, blk), lambda b, h, t, qt, kt: (b * heads + h, 0, kt[t])),
        ],
        out_specs=pl.BlockSpec((blk, dh), lambda b, h, t, qt, kt: (b * nq + qt[t], h)),
        scratch_shapes=[pltpu.VMEM((blk, LANES), F32), pltpu.VMEM((blk, LANES), F32),
                        pltpu.VMEM((blk, dh), F32), pltpu.VMEM((blk, LANES), F32)],
    )
    return pl.pallas_call(
        kernel,
        grid_spec=grid_spec,
        out_shape=jax.ShapeDtypeStruct((n, heads * dh), F32),
        compiler_params=_cparams(("parallel", "parallel", "arbitrary"), VMEM_LIMIT),
        name="fox_attention",
    )(qi_tab, ki_tab, proj, proj, proj, cum_col, cum_row)


def _s5_taps_kernel(wk_ref, cc_ref, o_ref):
    o_ref[0] = _dot(wk_ref[0], cc_ref[0], HIGHEST)


def _s5_kernel_taps(wk, cc):
    g, th, p2 = wk.shape
    h = cc.shape[2]
    return pl.pallas_call(
        _s5_taps_kernel,
        grid=(g,),
        in_specs=[pl.BlockSpec((1, th, p2), lambda i: (i, 0, 0)),
                  pl.BlockSpec((1, p2, h), lambda i: (i, 0, 0))],
        out_specs=pl.BlockSpec((1, th, h), lambda i: (i, 0, 0)),
        out_shape=jax.ShapeDtypeStruct((g, th, h), F32),
        compiler_params=_cparams(("parallel",)),
        name="s5_taps",
    )(wk, cc)


def _s5_fused_kernel(u_ref, mi_ref, ms_ref, mo_ref, a1_ref, a2_ref, d_ref, y_ref, tmp_sc, us_sc,
                     *, nlev, chunks, p, t):
    h = SSM_GROUP
    gpb = LANES // h
    n = u_ref.shape[0]
    r = n // t
    nblk = t * h // LANES
    tpb = LANES // h
    rc = tmp_sc.shape[0] // t

    def regroup(c, carry):
        tmp_sc[...] = u_ref[pl.ds(pl.multiple_of(c * rc * t, rc * t), rc * t), :].astype(F32)
        for tau in range(t):
            us_sc[tau, pl.ds(pl.multiple_of(c * rc, rc), rc), :] = tmp_sc[pl.ds(tau, rc, stride=t), :]
        return carry

    lax.fori_loop(0, r // rc, regroup, 0)
    grp = lax.broadcasted_iota(I32, (r, LANES), 1) // h
    row = lax.broadcasted_iota(I32, (r, 2 * p), 0) % chunks

    def group(g, carry):
        blocks = []
        for ob in range(nblk):
            acc = jnp.zeros((r, LANES), F32)
            for q in range(tpb):
                s = us_sc[ob * tpb + q]
                acc = jnp.where(grp == q, pltpu.roll(s, ((q - g) * h) % LANES, axis=1), acc)
            blocks.append(acc)
        uf = jnp.concatenate(blocks, axis=1)
        u = uf.astype(BF16)
        s = _dot(u, ms_ref[g])
        for j in range(nlev):
            d = 1 << j
            sh = jnp.where(row >= d, pltpu.roll(s, d, axis=0), 0.0)
            sw = pltpu.roll(sh, p, axis=1)
            s = s + a1_ref[g, j:j + 1, :] * sh + a2_ref[g, j:j + 1, :] * sw
        sp = jnp.where(row >= 1, pltpu.roll(s, 1, axis=0), 0.0)
        y = _dot(u, mi_ref[g]) + _dot(sp.astype(BF16), mo_ref[g]) + d_ref[g] * u.astype(F32)
        def scatter_lanes(first):
            for ob in range(nblk):
                yb = y[:, ob * LANES:(ob + 1) * LANES]
                for q in range(tpb):
                    tau = ob * tpb + q
                    moved = pltpu.roll(yb, ((g - q) * h) % LANES, axis=1)
                    y_ref[tau] = jnp.where(grp == g, moved, 0.0 if first else y_ref[tau])

        @pl.when(g == 0)
        def _():
            scatter_lanes(True)

        @pl.when(g > 0)
        def _():
            scatter_lanes(False)

        return carry

    lax.fori_loop(0, gpb, group, 0)


def _s5_fused(u, mi, ms, mo, a1, a2, dsk, *, col_blocks, t, chunks):
    n = u.shape[0]
    gpb = LANES // SSM_GROUP
    th = t * SSM_GROUP
    p2 = ms.shape[2]
    nlev = a1.shape[1]
    r = n // t
    kernel = functools.partial(_s5_fused_kernel, nlev=nlev, chunks=chunks, p=p2 // 2, t=t)
    blk3 = lambda i: (i, 0, 0)
    return pl.pallas_call(
        kernel,
        grid=(col_blocks,),
        in_specs=[pl.BlockSpec((n, LANES), lambda i: (0, i)),
                  pl.BlockSpec((gpb, th, th), blk3),
                  pl.BlockSpec((gpb, th, p2), blk3),
                  pl.BlockSpec((gpb, p2, th), blk3),
                  pl.BlockSpec((gpb, nlev, p2), blk3),
                  pl.BlockSpec((gpb, nlev, p2), blk3),
                  pl.BlockSpec((gpb, 1, th), blk3)],
        out_specs=pl.BlockSpec((t, r, LANES), lambda i: (0, 0, i)),
        out_shape=jax.ShapeDtypeStruct((t, r, col_blocks * LANES), F32),
        scratch_shapes=[pltpu.VMEM((min(r, S5_REGROUP_ROWS) * t, LANES), F32), pltpu.VMEM((t, r, LANES), F32)],
        compiler_params=_cparams(("parallel",), VMEM_LIMIT),
        name="s5_mixer",
    )(u, mi, ms, mo, a1, a2, dsk)


def _s5_params(lam_re, lam_im, log_step, b_re, b_im, c_re, c_im, d_skip, t, chunks):
    g, p = lam_re.shape
    h = SSM_GROUP
    lr, li = lam_re.astype(F32), lam_im.astype(F32)
    dt = jnp.exp(log_step.astype(F32))[:, None]
    mag = jnp.exp(lr * dt)
    lb_re, lb_im = mag * jnp.cos(li * dt), mag * jnp.sin(li * dt)
    den = lr * lr + li * li
    nr, ni = lb_re - 1.0, lb_im
    coef_re = (nr * lr + ni * li) / den
    coef_im = (ni * lr - nr * li) / den
    br, bi = b_re.astype(F32), b_im.astype(F32)
    bb_re = coef_re[..., None] * br - coef_im[..., None] * bi
    bb_im = coef_re[..., None] * bi + coef_im[..., None] * br

    def power(k):
        return jnp.exp(lr * dt * k) * jnp.cos(li * dt * k), jnp.exp(lr * dt * k) * jnp.sin(li * dt * k)

    kk = jnp.arange(t + 1, dtype=F32)[:, None, None]
    pw_re, pw_im = power(kk)
    wb_re = pw_re[:t, ..., None] * bb_re - pw_im[:t, ..., None] * bb_im
    wb_im = pw_re[:t, ..., None] * bb_im + pw_im[:t, ..., None] * bb_re
    wk = jnp.concatenate([wb_re, wb_im], axis=2).transpose(1, 0, 3, 2)
    ms = wk[:, ::-1].reshape(g, t * h, 2 * p)
    cre, cim = c_re.astype(F32), c_im.astype(F32)
    cc = jnp.concatenate([cre.transpose(0, 2, 1), -cim.transpose(0, 2, 1)], axis=1)
    taps = _s5_kernel_taps(wk.reshape(g, t * h, 2 * p), cc)
    kt = taps.reshape(g, t, h, h).transpose(0, 2, 1, 3).reshape(g, h, t * h)
    kpad = jnp.concatenate([jnp.zeros_like(kt), kt], axis=2)
    mi = jnp.stack([kpad[:, :, (t - s) * h:(2 * t - s) * h] for s in range(t)], axis=1)
    mi = mi.reshape(g, t * h, t * h)
    cre_t, cim_t = cre.transpose(0, 2, 1)[:, :, None, :], cim.transpose(0, 2, 1)[:, :, None, :]
    are = pw_re[1:].transpose(1, 2, 0)[..., None]
    aim = pw_im[1:].transpose(1, 2, 0)[..., None]
    mo_re = cre_t * are - cim_t * aim
    mo_im = -(cre_t * aim + cim_t * are)
    mo = jnp.concatenate([mo_re, mo_im], axis=1).reshape(g, 2 * p, t * h)
    nlev = max(1, (chunks - 1).bit_length())
    lv = (float(t) * (2.0 ** jnp.arange(nlev, dtype=F32)))[None, :, None]
    ar, ai = _power_levels(lr, li, dt, lv)
    a1 = jnp.concatenate([ar, ar], axis=2)
    a2 = jnp.concatenate([-ai, ai], axis=2)
    dsk = jnp.tile(d_skip.astype(F32).reshape(g, 1, h), (1, 1, t))
    return mi.astype(BF16), ms.astype(BF16), mo.astype(BF16), a1, a2, dsk


def _power_levels(lr, li, dt, lv):
    e = (lr * dt)[:, None, :] * lv
    w = (li * dt)[:, None, :] * lv
    return jnp.exp(e) * jnp.cos(w), jnp.exp(e) * jnp.sin(w)


def _mix_out_kernel(y_ref, att_ref, x_ref, wglu_ref, bglu_ref, gs_ref, ga_ref, woa_ref, wos_ref, o_ref, y_sc):
    t = y_ref.shape[0]
    for rl in range(y_ref.shape[1]):
        y_sc[rl * t:(rl + 1) * t, :] = y_ref[:, rl, :]
    y = jax.nn.gelu(y_sc[...])
    z = _dot(y.astype(BF16), wglu_ref[...]) + bglu_ref[...]
    ssm = y * jax.nn.sigmoid(z)
    na = _rms(att_ref[...], ga_ref[...]).astype(BF16)
    ns = _rms(ssm, gs_ref[...]).astype(BF16)
    o_ref[...] = x_ref[...] + _dot(na, woa_ref[...]) + _dot(ns, wos_ref[...])


def _mix_out(ypre, att, x, w_glu, b_glu, g_ssm, g_att, wo_a, wo_s, tm):
    n, d = x.shape
    t, _, sw = ypre.shape
    aw = att.shape[1]
    tm = min(tm, n)
    row = lambda i: (i, 0)
    fix = lambda i: (0, 0)
    once = dict(pipeline_mode=pl.Buffered(1))
    return pl.pallas_call(
        _mix_out_kernel,
        grid=(n // tm,),
        in_specs=[pl.BlockSpec((t, tm // t, sw), lambda i: (0, i, 0)),
                  pl.BlockSpec((tm, aw), row), pl.BlockSpec((tm, d), row),
                  pl.BlockSpec((sw, sw), fix, **once), pl.BlockSpec((1, sw), fix),
                  pl.BlockSpec((1, sw), fix), pl.BlockSpec((1, aw), fix),
                  pl.BlockSpec((aw, d), fix, **once), pl.BlockSpec((sw, d), fix, **once)],
        out_specs=pl.BlockSpec((tm, d), row),
        out_shape=jax.ShapeDtypeStruct((n, d), F32),
        scratch_shapes=[pltpu.VMEM((tm, sw), F32)],
        compiler_params=_cparams(("parallel",), VMEM_LIMIT),
        name="mix_out",
    )(ypre, att, x, w_glu, b_glu.reshape(1, sw), g_ssm.reshape(1, sw), g_att.reshape(1, aw), wo_a, wo_s)


def _cross_kernel(h_ref, g_ref, wq_ref, kv_ref, wo_ref, o_ref, *, heads):
    h = h_ref[...]
    d = h.shape[1]
    dh = d // heads
    q = _dot(_rms(h, g_ref[...]).astype(BF16), wq_ref[...]).astype(BF16)
    out = h
    for hd in range(heads):
        lo = hd * dh
        s = _dot_t(q[:, lo:lo + dh], kv_ref[:, lo:lo + dh]) * dh ** -0.5
        e = jnp.exp(s - jnp.max(s, axis=1, keepdims=True))
        p = e / jnp.sum(e, axis=1, keepdims=True)
        oh = _dot(p.astype(BF16), kv_ref[:, d + lo:d + lo + dh])
        out = out + _dot(oh.astype(BF16), wo_ref[lo:lo + dh, :])
    o_ref[...] = out


def _cross_attention(h, g, wq, kv, wo, *, batch, mem_len, tm):
    n, d = h.shape
    tm = min(tm, n // batch)
    tiles = n // batch // tm
    fix = lambda i: (0, 0)
    once = dict(pipeline_mode=pl.Buffered(1))
    return pl.pallas_call(
        functools.partial(_cross_kernel, heads=CROSS_HEADS),
        grid=(n // tm,),
        in_specs=[pl.BlockSpec((tm, d), lambda i: (i, 0)),
                  pl.BlockSpec((1, d), fix),
                  pl.BlockSpec((d, d), fix, **once),
                  pl.BlockSpec((mem_len, 2 * d), lambda i: (i // tiles, 0)),
                  pl.BlockSpec((d, d), fix, **once)],
        out_specs=pl.BlockSpec((tm, d), lambda i: (i, 0)),
        out_shape=jax.ShapeDtypeStruct((n, d), F32),
        compiler_params=_cparams(("parallel",), VMEM_LIMIT),
        name="cross_attention",
    )(h, g.reshape(1, d), wq, kv, wo)


def _router_kernel(h_ref, g_ref, wr_ref, br_ref, xm_ref, idx_ref, gate_ref, rank_ref, cnt_ref, carry_sc):
    @pl.when(pl.program_id(0) == 0)
    def _():
        carry_sc[...] = jnp.zeros(carry_sc.shape, F32)

    hm = _rms(h_ref[...], g_ref[...])
    xm_ref[...] = hm
    logits = _dot(hm, wr_ref[...], HIGHEST) + br_ref[...]
    tm = logits.shape[0]
    lane = lax.broadcasted_iota(I32, logits.shape, 1).astype(F32)
    work = logits
    vals, idxs = [], []
    for _ in range(TOP_K):
        m = jnp.max(work, axis=1, keepdims=True)
        am = jnp.min(jnp.where(work == m, lane, float(LANES)), axis=1, keepdims=True)
        vals.append(m)
        idxs.append(am)
        work = jnp.where(lane == am, NEG_BIG, work)
    es = [jnp.exp(v - vals[0]) for v in vals]
    den = es[0] + es[1] + es[2] + es[3]
    sel = jnp.zeros(logits.shape, F32)
    for am in idxs:
        sel = sel + (lane == am).astype(F32)
    r = lax.broadcasted_iota(I32, (tm, tm), 0)
    c = lax.broadcasted_iota(I32, (tm, tm), 1)
    before = (c < r).astype(BF16)
    rank_all = carry_sc[0:1, :] + _dot(before, sel.astype(BF16))
    idx_o = jnp.zeros(logits.shape, F32)
    gate_o = jnp.zeros(logits.shape, F32)
    rank_o = jnp.zeros(logits.shape, F32)
    for j in range(TOP_K):
        slot = lane == float(j)
        rk = jnp.sum(jnp.where(lane == idxs[j], rank_all, 0.0), axis=1, keepdims=True)
        idx_o = jnp.where(slot, idxs[j], idx_o)
        gate_o = jnp.where(slot, es[j] / den, gate_o)
        rank_o = jnp.where(slot, rk, rank_o)
    idx_ref[...] = idx_o.astype(I32)
    gate_ref[...] = gate_o
    rank_ref[...] = rank_o.astype(I32)
    carry_sc[...] = carry_sc[...] + jnp.sum(sel, axis=0, keepdims=True)
    cnt_ref[...] = carry_sc[...].astype(I32)


def _router(h, g, wr, br, tm):
    n, d = h.shape
    tm = min(tm, n)
    row = lambda i: (i, 0)
    fix = lambda i: (0, 0)
    return pl.pallas_call(
        _router_kernel,
        grid=(n // tm,),
        in_specs=[pl.BlockSpec((tm, d), row), pl.BlockSpec((1, d), fix),
                  pl.BlockSpec((d, LANES), fix), pl.BlockSpec((1, LANES), fix)],
        out_specs=[pl.BlockSpec((tm, d), row), pl.BlockSpec((tm, LANES), row),
                   pl.BlockSpec((tm, LANES), row), pl.BlockSpec((tm, LANES), row),
                   pl.BlockSpec((8, LANES), fix)],
        out_shape=[jax.ShapeDtypeStruct((n, d), F32), jax.ShapeDtypeStruct((n, LANES), I32),
                   jax.ShapeDtypeStruct((n, LANES), F32), jax.ShapeDtypeStruct((n, LANES), I32),
                   jax.ShapeDtypeStruct((8, LANES), I32)],
        scratch_shapes=[pltpu.VMEM((8, LANES), F32)],
        compiler_params=_cparams(("arbitrary",), VMEM_LIMIT),
        name="router",
    )(h, g.reshape(1, d), wr, br)


def _row_copy(src_hbm, row, dst, slot, sem):
    return pltpu.make_async_copy(src_hbm.at[pl.ds(row, 1), :], dst.at[pl.ds(slot, 1), :], sem)


def _slab_copy(src_hbm, dst, slot, rows, sem):
    return pltpu.make_async_copy(src_hbm.at[pl.ds(0, rows), :], dst.at[pl.ds(slot, rows), :], sem)


def _expert_kernel(sbe_ref, sbn_ref, sbf_ref, nused_ref, tok_ref, tokn_ref, xm_hbm, wgu_hbm, wdn_hbm, bgu_ref, bdn_ref,
                    o_ref, xg_ref, xb_ref, wg_res, wl_res, wd_res, stg_g, stg_l, stg_d, gsem, wsem,
                    *, nf, tf, n_sb):
    i = pl.program_id(0)
    slot = i % 2
    e = sbe_ref[i]
    nsub = (sbn_ref[i] + MOE_SUB - 1) // MOE_SUB
    first = sbf_ref[i]
    sb_rows, d = xb_ref.shape
    subs = sb_rows // MOE_SUB
    step_rows = sb_rows // nf

    def weight_copies(expert, j, s):
        lo = pl.multiple_of(j * tf, tf)
        hi = pl.multiple_of(nf * tf + j * tf, tf)
        return (pltpu.make_async_copy(wgu_hbm.at[expert, :, pl.ds(lo, tf)], stg_g.at[s], wsem.at[0, s]),
                pltpu.make_async_copy(wgu_hbm.at[expert, :, pl.ds(hi, tf)], stg_l.at[s], wsem.at[1, s]),
                pltpu.make_async_copy(wdn_hbm.at[expert, pl.ds(lo, tf), :], stg_d.at[s], wsem.at[2, s]))

    def start_first_tiles(expert):
        for s in range(2):
            for cp in weight_copies(expert, s, s):
                cp.start()

    nxt = jnp.minimum(i + 1, n_sb - 1)
    nsub_next = jnp.where(i + 1 < n_sb, (sbn_ref[nxt] + MOE_SUB - 1) // MOE_SUB, 0)
    next_is_first = jnp.logical_and(i + 1 < n_sb, sbf_ref[nxt] == 1)

    @pl.when(i == 0)
    def _():
        start_first_tiles(e)

        def issue(r8, c):
            for u in range(ISSUE_UNROLL):
                r = r8 * ISSUE_UNROLL + u
                _row_copy(xm_hbm, tok_ref[0, 0, r], xg_ref.at[0], r, gsem).start()
            return c

        lax.fori_loop(0, nsub * (MOE_SUB // ISSUE_UNROLL), issue, 0)

    for s in range(subs):
        @pl.when(s < nsub)
        def _():
            _slab_copy(xm_hbm, xg_ref.at[slot], s * MOE_SUB, MOE_SUB, gsem).wait()

    for s in range(subs):
        @pl.when(s < nsub)
        def _():
            rows = pl.ds(s * MOE_SUB, MOE_SUB)
            xb_ref[rows, :] = xg_ref[slot, rows, :].astype(BF16)

    o_ref[...] = jnp.broadcast_to(bdn_ref[...], o_ref.shape)

    @pl.when(jnp.logical_and(next_is_first, first == 0))
    def _():
        start_first_tiles(sbe_ref[nxt])

    def gather_next(j):
        @pl.when((j * step_rows) // MOE_SUB < nsub_next)
        def _():
            for u in range(step_rows):
                r = j * step_rows + u
                _row_copy(xm_hbm, tokn_ref[0, 0, r], xg_ref.at[1 - slot], r, gsem).start()

    def ffn(j, rows):
        x = xb_ref[rows, :]
        g = jnp.minimum(_dot(x, wg_res[j]) + bgu_ref[j], SWIGLU_LIMIT)
        lin = jnp.clip(_dot(x, wl_res[j]) + bgu_ref[nf + j], -SWIGLU_LIMIT, SWIGLU_LIMIT)
        act = (g * jax.nn.sigmoid(SWIGLU_ALPHA * g) * (lin + 1.0)).astype(BF16)
        for c in range(0, d, MOE_NCHUNK):
            cols = pl.ds(c, min(MOE_NCHUNK, d - c))
            o_ref[rows, cols] = o_ref[rows, cols] + _dot(act, wd_res[j, :, cols])

    def tile(j, carry):
        s = j % 2

        @pl.when(first == 1)
        def _():
            for cp in weight_copies(e, j, s):
                cp.wait()
            wg_res[j] = stg_g[s].astype(BF16)
            wl_res[j] = stg_l[s].astype(BF16)
            wd_res[j] = stg_d[s].astype(BF16)

            @pl.when(j + 2 < nf)
            def _():
                for cp in weight_copies(e, j + 2, s):
                    cp.start()

        gather_next(j)
        for k in range(1, subs + 1):
            @pl.when(nsub == k)
            def _():
                ffn(j, pl.ds(0, k * MOE_SUB))

        return carry

    lax.fori_loop(0, nf, tile, 0)

    @pl.when(jnp.logical_and(next_is_first, first == 1))
    def _():
        start_first_tiles(sbe_ref[nxt])


def _experts(xm, row_tok, sb_e, sb_n, sb_f, n_used, w_gu, b_gu, w_dn, b_dn):
    n_sb = sb_e.shape[0]
    sb_rows = MOE_SB
    e, d, f2 = w_gu.shape
    f = f2 // 2
    tf = min(MOE_FTILE, f)
    nf = f // tf
    assert nf >= 2 and sb_rows % nf == 0 and MOE_SUB % (sb_rows // nf) == 0
    tok3 = row_tok.reshape(n_sb, 1, sb_rows)
    grid_spec = pltpu.PrefetchScalarGridSpec(
        num_scalar_prefetch=4,
        grid=(n_sb,),
        in_specs=[
            pl.BlockSpec((1, 1, sb_rows), lambda i, sbe, sbn, sbf, nu: (i, 0, 0), memory_space=pltpu.SMEM),
            pl.BlockSpec((1, 1, sb_rows), lambda i, sbe, sbn, sbf, nu: (jnp.minimum(i + 1, n_sb - 1), 0, 0),
                         memory_space=pltpu.SMEM),
            pl.BlockSpec(memory_space=pl.ANY),
            pl.BlockSpec(memory_space=pl.ANY),
            pl.BlockSpec(memory_space=pl.ANY),
            pl.BlockSpec((None, 2 * nf, 1, tf), lambda i, sbe, sbn, sbf, nu: (sbe[i], 0, 0, 0)),
            pl.BlockSpec((None, 1, d), lambda i, sbe, sbn, sbf, nu: (sbe[i], 0, 0)),
        ],
        out_specs=pl.BlockSpec((sb_rows, d), lambda i, sbe, sbn, sbf, nu: (jnp.where(i < nu[0], i, n_sb), 0)),
        scratch_shapes=[pltpu.VMEM((2, sb_rows, d), F32), pltpu.VMEM((sb_rows, d), BF16),
                        pltpu.VMEM((nf, d, tf), BF16), pltpu.VMEM((nf, d, tf), BF16), pltpu.VMEM((nf, tf, d), BF16),
                        pltpu.VMEM((2, d, tf), F32), pltpu.VMEM((2, d, tf), F32), pltpu.VMEM((2, tf, d), F32),
                        pltpu.SemaphoreType.DMA, pltpu.SemaphoreType.DMA((3, 2))],
    )
    return pl.pallas_call(
        functools.partial(_expert_kernel, nf=nf, tf=tf, n_sb=n_sb),
        grid_spec=grid_spec,
        out_shape=jax.ShapeDtypeStruct(((n_sb + 1) * sb_rows, d), F32),
        compiler_params=_cparams(("arbitrary",), MOE_VMEM_LIMIT),
        name="experts",
    )(sb_e, sb_n, sb_f, n_used, tok3, tok3, xm, w_gu, w_dn, b_gu.reshape(e, 2 * nf, 1, tf), b_dn.reshape(e, 1, d))


def _combine_kernel(dest_ref, gate_ref, h_ref, g_ref, y_hbm, o_ref, buf_ref, sem):
    tm = h_ref.shape[0]

    tok_unroll = ISSUE_UNROLL // TOP_K

    def issue(t2, c):
        for u in range(tok_unroll):
            t = t2 * tok_unroll + u
            for k in range(TOP_K):
                _row_copy(y_hbm, dest_ref[0, 0, t * TOP_K + k], buf_ref.at[k], t, sem).start()
        return c

    lax.fori_loop(0, tm // tok_unroll, issue, 0)
    acc = h_ref[...]
    gates = gate_ref[...]
    for k in range(TOP_K):
        _slab_copy(y_hbm, buf_ref.at[k], 0, tm, sem).wait()
    for k in range(TOP_K):
        acc = acc + gates[:, k:k + 1] * buf_ref[k]
    o_ref[...] = _rms(acc, g_ref[...])


def _combine(dest, gates, h, g, yrows, tm):
    n, d = h.shape
    tm = min(tm, n)
    return pl.pallas_call(
        _combine_kernel,
        grid=(n // tm,),
        in_specs=[pl.BlockSpec((1, 1, tm * TOP_K), lambda i: (i, 0, 0), memory_space=pltpu.SMEM),
                  pl.BlockSpec((tm, LANES), lambda i: (i, 0)),
                  pl.BlockSpec((tm, d), lambda i: (i, 0)),
                  pl.BlockSpec((1, d), lambda i: (0, 0)),
                  pl.BlockSpec(memory_space=pl.ANY)],
        out_specs=pl.BlockSpec((tm, d), lambda i: (i, 0)),
        out_shape=jax.ShapeDtypeStruct((n, d), F32),
        scratch_shapes=[pltpu.VMEM((TOP_K, tm, d), F32), pltpu.SemaphoreType.DMA],
        compiler_params=_cparams(("arbitrary",), VMEM_LIMIT),
        name="combine",
    )(dest.reshape(n // tm, 1, tm * TOP_K), gates, h, g.reshape(1, d), yrows)


def _layer(h, mem2d, p, *, batch, length):
    n, d = h.shape
    g_cnt, p_state = p["lam_re"].shape
    sw = g_cnt * SSM_GROUP
    aw = d - sw
    heads = aw // ATTN_HEAD_DIM
    assert p["w_in"].shape[1] == sw + 3 * aw + heads and sw % LANES == 0 and 2 * p_state == LANES

    w_in = p["w_in"]
    cols = sw + 3 * aw
    q_scale = ATTN_HEAD_DIM ** -0.5 * LOG2E
    cs = jnp.concatenate([jnp.ones((sw,), F32), jnp.full((aw,), q_scale, F32), jnp.ones((2 * aw,), F32)])
    w_f = jnp.pad(w_in[:, cols:], ((0, 0), (0, LANES - heads)))
    proj, flog = _in_proj(h, p["g_mix"], w_in[:, :cols].astype(BF16), cs, w_f, 1024, 1024)

    f_rows = flog[:, :heads].reshape(batch, length, heads).transpose(0, 2, 1).reshape(batch * heads, length)
    bias_rows = jnp.tile(p["b_f"].astype(F32), batch).reshape(batch * heads, 1)
    cum_rows = _fgate_cumsum(f_rows, bias_rows)
    cum_col = cum_rows.reshape(batch, heads, length).transpose(0, 2, 1).reshape(n, heads)

    blk = lambda c: c // ATTN_HEAD_DIM
    att = _fox_attention(proj, cum_col, cum_rows.reshape(batch * heads, 1, length), batch=batch, length=length,
                         heads=heads, q_col=blk(sw), k_col=blk(sw + aw), v_col=blk(sw + 2 * aw))

    t = min(S5_CHUNK, length)
    chunks = length // t
    mi, ms, mo, a1, a2, dsk = _s5_params(p["lam_re"], p["lam_im"], p["log_step"], p["b_re"], p["b_im"],
                                         p["c_re"], p["c_im"], p["d_skip"], t, chunks)
    ypre = _s5_fused(proj, mi, ms, mo, a1, a2, dsk, col_blocks=sw // LANES, t=t, chunks=chunks)

    w_out = p["w_out"].astype(BF16)
    h = _mix_out(ypre, att, h, p["w_glu"].astype(BF16), p["b_glu"], p["g_ssm_out"], p["g_attn_out"],
                 w_out[:aw], w_out[aw:], 256)

    mem_len = mem2d.shape[0] // batch
    kv = _norm_matmul(mem2d, p["g_mem"], p["w_ckv"].astype(BF16), BF16, 512, 512)
    h = _cross_attention(h, p["g_cross"], p["w_cq"].astype(BF16), kv, p["w_co"].astype(BF16),
                         batch=batch, mem_len=mem_len, tm=512)

    dest, gates, yrows = _moe(h, p)
    return dest, gates, h, yrows


def _moe(h, p):
    n = h.shape[0]
    n_exp = p["w_router"].shape[1]
    wr = jnp.pad(p["w_router"].astype(F32), ((0, 0), (0, LANES - n_exp)))
    br = jnp.pad(p["b_router"].astype(F32), (0, LANES - n_exp), constant_values=NEG_BIG).reshape(1, LANES)
    xm, idx, gates, rank, cnt = _router(h, p["g_moe"], wr, br, 256)

    sb_rows = MOE_SB
    n_sb = (n * TOP_K) // sb_rows + n_exp
    counts = cnt[0, :n_exp]
    padded = (counts + sb_rows - 1) // sb_rows * sb_rows
    pend = jnp.cumsum(padded)
    pstart = pend - padded
    idx_k, rank_k = idx[:, :TOP_K], rank[:, :TOP_K]
    onehot = idx_k[..., None] == jnp.arange(n_exp, dtype=I32)
    dest = rank_k + jnp.sum(jnp.where(onehot, pstart, 0), axis=-1)
    tok = jnp.broadcast_to(jnp.arange(n, dtype=I32)[:, None], (n, TOP_K))
    row_tok = jnp.zeros((n_sb * sb_rows,), I32).at[dest.reshape(-1)].set(tok.reshape(-1), unique_indices=True)
    n_used = (pend[-1] // sb_rows).astype(I32)
    sb_start = jnp.arange(n_sb, dtype=I32) * sb_rows
    sb_e = jnp.clip(jnp.sum(sb_start[:, None] >= pend[None, :], axis=1), 0, n_exp - 1).astype(I32)
    used = jnp.arange(n_sb, dtype=I32) < n_used
    sb_n = jnp.where(used, jnp.clip(counts[sb_e] - (sb_start - pstart[sb_e]), 0, sb_rows), 0).astype(I32)
    sb_f = jnp.logical_and(used, sb_start == pstart[sb_e]).astype(I32)
    sb_e = jnp.where(used, sb_e, sb_e[jnp.maximum(n_used - 1, 0)])

    yrows = _experts(xm, row_tok, sb_e, sb_n, sb_f, n_used.reshape(1), p["w_gu"], p["b_gu"], p["w_dn"], p["b_dn"])
    return dest, gates, yrows


def kernel(x, mem, g_mix, w_in, b_f, lam_re, lam_im, log_step, b_re, b_im, c_re, c_im, d_skip, w_glu, b_glu, g_attn_out, g_ssm_out, w_out, g_cross, g_mem, w_cq, w_ckv, w_co, g_moe, w_router, b_router, w_gu, b_gu, w_dn, b_dn, g_final):
    batch, length, d = x.shape
    depth = g_mix.shape[0]
    assert depth == 1, "the combine kernel applies the final norm, so exactly one layer is supported"
    names = ("g_mix", "w_in", "b_f", "lam_re", "lam_im", "log_step", "b_re", "b_im", "c_re", "c_im", "d_skip",
             "w_glu", "b_glu", "g_attn_out", "g_ssm_out", "w_out", "g_cross", "g_mem", "w_cq", "w_ckv", "w_co",
             "g_moe", "w_router", "b_router", "w_gu", "b_gu", "w_dn", "b_dn")
    vals = (g_mix, w_in, b_f, lam_re, lam_im, log_step, b_re, b_im, c_re, c_im, d_skip, w_glu, b_glu,
            g_attn_out, g_ssm_out, w_out, g_cross, g_mem, w_cq, w_ckv, w_co, g_moe, w_router, b_router,
            w_gu, b_gu, w_dn, b_dn)
    params = {k: v[0] for k, v in zip(names, vals)}
    h = x.reshape(batch * length, d)
    mem2d = mem.reshape(-1, d)
    dest, gates, h, yrows = _layer(h, mem2d, params, batch=batch, length=length)
    out = _combine(dest, gates, h, g_final, yrows, 512)
    return out.reshape(batch, length, d)
```
